```python
import jax, jax.numpy as jnp
from jax import lax
import numpy as np

D_MODEL = 1024
BATCH = 2
SEQ = 8192
DEPTH = 2

MIX_WIDTH = D_MODEL
ATT_HEADS = 8
ATT_HEAD_DIM = 64
ATT_W = ATT_HEADS * ATT_HEAD_DIM
ROPE_DIM = ATT_HEAD_DIM // 4
ROPE_THETA = 500000.0
MOBA_BLOCK = 256
MOBA_TOPK = 3
Q_BLOCK = 64
MLSTM_HEADS = 4
MLSTM_QK_DIM = 64
MLSTM_V_DIM = 128
ML_QK_W = MLSTM_HEADS * MLSTM_QK_DIM
ML_V_W = MLSTM_HEADS * MLSTM_V_DIM
MLSTM_CHUNK = 64
D_FF = 4 * D_MODEL
EPS = 1e-6
IN_SPLITS = (ATT_W, ATT_W, ATT_W, ML_QK_W, ML_QK_W, ML_V_W, ML_V_W, MLSTM_HEADS, MLSTM_HEADS)
IN_WIDTH = 3 * ATT_W + 2 * ML_QK_W + 2 * ML_V_W + 2 * MLSTM_HEADS

kernel_name = "hybrid_moba_mlstm_parallel_heads"


def rms_norm(x, g):
    xf = x.astype(jnp.float32)
    y = xf * lax.rsqrt(jnp.mean(xf * xf, axis=-1, keepdims=True) + EPS) * g.astype(jnp.float32)
    return y.astype(x.dtype)


def partial_rope(x, positions):
    inv_freq = ROPE_THETA ** (-jnp.arange(0, ROPE_DIM, 2, dtype=jnp.float32) / ROPE_DIM)
    ang = positions.astype(jnp.float32)[:, None, :, None] * inv_freq
    cos = jnp.cos(jnp.concatenate([ang, ang], axis=-1))
    sin = jnp.sin(jnp.concatenate([ang, ang], axis=-1))
    xr, xp = x[..., :ROPE_DIM], x[..., ROPE_DIM:]
    x1, x2 = xr[..., :ROPE_DIM // 2], xr[..., ROPE_DIM // 2:]
    xr = xr * cos + jnp.concatenate([-x2, x1], axis=-1) * sin
    return jnp.concatenate([xr, xp], axis=-1)


def moba_attention(q, k, v):
    B, H, S, Dh = q.shape
    nb = -(-S // MOBA_BLOCK)
    pad = nb * MOBA_BLOCK - S
    kp = jnp.pad(k, ((0, 0), (0, 0), (0, pad), (0, 0)))
    vp = jnp.pad(v, ((0, 0), (0, 0), (0, pad), (0, 0)))
    kb = kp.reshape(B, H, nb, MOBA_BLOCK, Dh)
    vb = vp.reshape(B, H, nb, MOBA_BLOCK, Dh)
    k_mean = jnp.mean(kb, axis=3)
    n_sel = min(MOBA_TOPK, nb)
    scale = Dh ** -0.5
    blk_ids = jnp.arange(nb)
    gather = jax.vmap(jax.vmap(lambda blocks, ids: blocks[ids]))

    def one_block(c):
        q0 = c * Q_BLOCK
        own = q0 // MOBA_BLOCK
        qc = lax.dynamic_slice_in_dim(q, q0, Q_BLOCK, axis=2)
        qpos = q0 + jnp.arange(Q_BLOCK)
        gate = jnp.einsum('bhqd,bhnd->bhqn', qc, k_mean)
        gate = jnp.where(blk_ids < own, gate, -jnp.inf)
        _, sel = lax.top_k(gate, n_sel)
        valid = sel < own
        k_sel = gather(kb, sel)
        v_sel = gather(vb, sel)
        s_sel = jnp.einsum('bhqd,bhqnkd->bhqnk', qc, k_sel) * scale
        s_sel = jnp.where(valid[..., None], s_sel, -jnp.inf).reshape(B, H, Q_BLOCK, n_sel * MOBA_BLOCK)
        k_own = lax.dynamic_slice_in_dim(kp, own * MOBA_BLOCK, MOBA_BLOCK, axis=2)
        v_own = lax.dynamic_slice_in_dim(vp, own * MOBA_BLOCK, MOBA_BLOCK, axis=2)
        kpos = own * MOBA_BLOCK + jnp.arange(MOBA_BLOCK)
        s_own = jnp.einsum('bhqd,bhkd->bhqk', qc, k_own) * scale
        s_own = jnp.where(kpos[None, :] <= qpos[:, None], s_own, -jnp.inf)
        p = jax.nn.softmax(jnp.concatenate([s_sel, s_own], axis=-1).astype(jnp.float32), axis=-1)
        p_sel = p[..., :n_sel * MOBA_BLOCK].reshape(B, H, Q_BLOCK, n_sel, MOBA_BLOCK)
        p_own = p[..., n_sel * MOBA_BLOCK:]
        return (jnp.einsum('bhqnk,bhqnkd->bhqd', p_sel, v_sel)
                + jnp.einsum('bhqk,bhkd->bhqd', p_own, v_own))

    out = lax.map(one_block, jnp.arange(S // Q_BLOCK))
    return jnp.transpose(out, (1, 2, 0, 3, 4)).reshape(B, H, S, Dh)


def mlstm_chunkwise(q, k, v, ig, lf):
    B, H, S, dk = q.shape
    dv = v.shape[-1]
    L = MLSTM_CHUNK
    nc = S // L

    def chunks(a):
        return jnp.moveaxis(a.reshape((B, H, nc, L) + a.shape[3:]), 2, 0)

    causal = jnp.tril(jnp.ones((L, L), dtype=bool))

    def step(carry, xs):
        C, n, m = carry
        qc, kc, vc, ic, fc = xs
        b = jnp.cumsum(fc, axis=-1)
        d = b[..., :, None] - b[..., None, :] + ic[..., None, :]
        d = jnp.where(causal, d, -jnp.inf)
        m_inter = b + m[..., None]
        m_t = jnp.maximum(m_inter, jnp.max(d, axis=-1))
        s = jnp.einsum('bhtd,bhsd->bhts', qc, kc) * jnp.exp(d - m_t[..., None])
        decay = jnp.exp(m_inter - m_t)
        num = (jnp.einsum('bhts,bhsv->bhtv', s, vc)
               + decay[..., None] * jnp.einsum('bhtd,bhdv->bhtv', qc, C))
        den = jnp.sum(s, axis=-1) + decay * jnp.einsum('bhtd,bhd->bht', qc, n)
        h = num / jnp.maximum(jnp.abs(den), jnp.exp(-m_t))[..., None]
        b_last = b[..., -1]
        g = b_last[..., None] - b + ic
        m_new = jnp.maximum(b_last + m, jnp.max(g, axis=-1))
        carry_decay = jnp.exp(b_last + m - m_new)
        wg = jnp.exp(g - m_new[..., None])
        C_new = carry_decay[..., None, None] * C + jnp.einsum('bhs,bhsd,bhsv->bhdv', wg, kc, vc)
        n_new = carry_decay[..., None] * n + jnp.einsum('bhs,bhsd->bhd', wg, kc)
        return (C_new, n_new, m_new), h

    init = (jnp.zeros((B, H, dk, dv), jnp.float32),
            jnp.zeros((B, H, dk), jnp.float32),
            jnp.zeros((B, H), jnp.float32))
    _, h = lax.scan(step, init, (chunks(q), chunks(k), chunks(v), chunks(ig), chunks(lf)))
    return jnp.moveaxis(h, 0, 2).reshape(B, H, S, dv)


def hybrid_layer(x, positions, g_mix_pre, w_in, b_igate, b_fgate, g_mlstm_out, w_out,
                 g_mix_post, g_mlp_pre, w_up, w_down, g_mlp_post):
    B, S, _ = x.shape
    hdn = rms_norm(x, g_mix_pre)
    proj = hdn @ w_in
    aq, ak, av, mq, mk, mv, mo, mi, mf = jnp.split(proj, [int(c) for c in np.cumsum(IN_SPLITS)[:-1]], axis=-1)

    def to_heads(t, nh):
        return t.reshape(B, S, nh, -1).transpose(0, 2, 1, 3).astype(jnp.float32)

    q_a = partial_rope(to_heads(aq, ATT_HEADS), positions)
    k_a = partial_rope(to_heads(ak, ATT_HEADS), positions)
    v_a = to_heads(av, ATT_HEADS)
    att = moba_attention(q_a, k_a, v_a)
    att = att.transpose(0, 2, 1, 3).reshape(B, S, ATT_W)

    q_m = to_heads(mq, MLSTM_HEADS)
    k_m = to_heads(mk, MLSTM_HEADS) * (MLSTM_QK_DIM ** -0.5)
    v_m = to_heads(mv, MLSTM_HEADS)
    ig = (mi.astype(jnp.float32) + b_igate.astype(jnp.float32)).transpose(0, 2, 1)
    lf = jax.nn.log_sigmoid(mf.astype(jnp.float32) + b_fgate.astype(jnp.float32)).transpose(0, 2, 1)
    hm = mlstm_chunkwise(q_m, k_m, v_m, ig, lf)
    hm = hm * lax.rsqrt(jnp.mean(hm * hm, axis=-1, keepdims=True) + EPS)
    hm = hm.transpose(0, 2, 1, 3).reshape(B, S, ML_V_W) * g_mlstm_out.astype(jnp.float32)
    hm = hm * jax.nn.sigmoid(mo.astype(jnp.float32))

    mixed = jnp.concatenate([att, hm], axis=-1).astype(x.dtype)
    x = x + rms_norm(mixed @ w_out, g_mix_post)

    u = rms_norm(x, g_mlp_pre) @ w_up
    u = jnp.square(jax.nn.relu(u))
    x = x + rms_norm(u @ w_down, g_mlp_post)
    return x


def setup_inputs(seed: int = 0) -> dict:
    key = jax.random.key(seed)
    ks = jax.random.split(key, 14)
    f32 = jnp.float32

    def gain(k, n):
        return 1.0 + 0.02 * jax.random.normal(k, (DEPTH, n), f32)

    x = jax.random.normal(ks[0], (BATCH, SEQ, D_MODEL), f32)
    positions = jnp.broadcast_to(jnp.arange(SEQ, dtype=jnp.int32), (BATCH, SEQ))
    g_mix_pre = gain(ks[1], D_MODEL)
    w_in = jax.random.normal(ks[2], (DEPTH, D_MODEL, IN_WIDTH), f32) * D_MODEL ** -0.5
    b_igate = 0.1 * jax.random.normal(ks[3], (DEPTH, MLSTM_HEADS), f32)
    b_fgate = (jnp.linspace(3.0, 6.0, MLSTM_HEADS, dtype=f32)[None, :]
               + 0.1 * jax.random.normal(ks[4], (DEPTH, MLSTM_HEADS), f32))
    g_mlstm_out = gain(ks[5], ML_V_W)
    w_out = jax.random.normal(ks[6], (DEPTH, MIX_WIDTH, D_MODEL), f32) * MIX_WIDTH ** -0.5
    g_mix_post = gain(ks[7], D_MODEL)
    g_mlp_pre = gain(ks[8], D_MODEL)
    w_up = jax.random.normal(ks[9], (DEPTH, D_MODEL, D_FF), f32) * D_MODEL ** -0.5
    w_down = jax.random.normal(ks[10], (DEPTH, D_FF, D_MODEL), f32) * D_FF ** -0.5
    g_mlp_post = gain(ks[11], D_MODEL)
    return {"x": x, "positions": positions, "g_mix_pre": g_mix_pre, "w_in": w_in,
            "b_igate": b_igate, "b_fgate": b_fgate, "g_mlstm_out": g_mlstm_out, "w_out": w_out,
            "g_mix_post": g_mix_post, "g_mlp_pre": g_mlp_pre, "w_up": w_up, "w_down": w_down,
            "g_mlp_post": g_mlp_post}


def reference(x, positions, g_mix_pre, w_in, b_igate, b_fgate, g_mlstm_out, w_out,
              g_mix_post, g_mlp_pre, w_up, w_down, g_mlp_post):
    for layer in range(DEPTH):
        x = hybrid_layer(x, positions, g_mix_pre[layer], w_in[layer], b_igate[layer], b_fgate[layer],
                         g_mlstm_out[layer], w_out[layer], g_mix_post[layer], g_mlp_pre[layer],
                         w_up[layer], w_down[layer], g_mlp_post[layer])
    return x
```

```python
import functools

import jax
import jax.numpy as jnp
import numpy as np
from jax import lax
from jax.experimental import pallas as pl
from jax.experimental.pallas import tpu as pltpu

F32 = jnp.float32
BF16 = jnp.bfloat16

D_MODEL = 1024
ATT_HEADS = 8
ATT_HEAD_DIM = 64
ATT_W = ATT_HEADS * ATT_HEAD_DIM
ROPE_DIM = ATT_HEAD_DIM // 4
ROPE_THETA = 500000.0
MOBA_BLOCK = 256
MOBA_TOPK = 3
MLSTM_HEADS = 4
MLSTM_QK_DIM = 64
MLSTM_V_DIM = 128
ML_QK_W = MLSTM_HEADS * MLSTM_QK_DIM
ML_V_W = MLSTM_HEADS * MLSTM_V_DIM
D_FF = 4 * D_MODEL
EPS = 1e-6

LANES = 128
ROW_TILE = MOBA_BLOCK
HEADS_PER_GROUP = 4
GROUP_W = HEADS_PER_GROUP * ATT_HEAD_DIM
MAIN_W = 3 * ATT_W + 2 * ML_QK_W + 2 * ML_V_W
GATE_ROWS = 16
FF_CHUNK = 512
VMEM_LIMIT = 56 * 1024 * 1024
NEG_INF = float("-inf")


def _rms(x, g):
    return x * lax.rsqrt(jnp.mean(x * x, axis=-1, keepdims=True) + EPS) * g


def _rope_tables_kernel(pos_ref, invf_ref, ma_ref, mb_ref, cos_ref, sa_ref, sb_ref):
    ang = pos_ref[...].astype(F32) * invf_ref[...]
    sin = jnp.sin(ang)
    cos_ref[...] = jnp.cos(ang)
    sa_ref[...] = sin * ma_ref[...]
    sb_ref[...] = sin * mb_ref[...]


def _rope_tables(positions):
    n = positions.size
    rows = 1024
    inv_freq = ROPE_THETA ** (-jnp.arange(0, ROPE_DIM, 2, dtype=F32) / ROPE_DIM)
    lane = np.arange(LANES) % ATT_HEAD_DIM
    half = ROPE_DIM // 2
    invf = jnp.where(lane < ROPE_DIM, inv_freq[lane % half], 0.0).astype(F32)[None, :]
    ma = jnp.asarray(np.where(lane < half, -1.0, 0.0), F32)[None, :]
    mb = jnp.asarray(np.where((lane >= half) & (lane < ROPE_DIM), 1.0, 0.0), F32)[None, :]
    tab = jax.ShapeDtypeStruct((n, LANES), F32)
    vec = pl.BlockSpec((1, LANES), lambda i: (0, 0))
    out = pl.BlockSpec((rows, LANES), lambda i: (i, 0))
    return pl.pallas_call(
        _rope_tables_kernel,
        grid=(n // rows,),
        in_specs=[pl.BlockSpec((rows, 1), lambda i: (i, 0)), vec, vec, vec],
        out_specs=[out, out, out],
        out_shape=[tab, tab, tab],
        name="rope_tables",
    )(positions.reshape(n, 1), invf, ma, mb)


def _scan_rows(v, op, ident):
    n = v.shape[0]
    rows = lax.broadcasted_iota(jnp.int32, v.shape, 0)
    s = 1
    while s < n:
        shifted = jnp.where(rows >= s, pltpu.roll(v, s, 0), ident)
        v = op(v, shifted)
        s *= 2
    return v


def _rope(t, cos, sa, sb):
    outs = []
    for c in range(t.shape[1] // LANES):
        tc = t[:, c * LANES:(c + 1) * LANES]
        outs.append(tc * cos + pltpu.roll(tc, LANES - ROPE_DIM // 2, 1) * sa
                    + pltpu.roll(tc, ROPE_DIM // 2, 1) * sb)
    return jnp.concatenate(outs, axis=1)


def _in_proj_kernel(x_ref, g_ref, w_ref, wg_ref, bg_ref, cos_ref, sa_ref, sb_ref,
                    qat_ref, ka_ref, vat_ref, qm_ref, kmt_ref, vm_ref, mo_ref, gc_ref, gt_ref):
    h = _rms(x_ref[...], g_ref[...]).astype(BF16)

    def proj(c0, n):
        return jnp.dot(h, w_ref[:, c0:c0 + n], preferred_element_type=F32)

    cos, sa, sb = cos_ref[...], sa_ref[...], sb_ref[...]
    scale = ATT_HEAD_DIM ** -0.5
    qa = _rope(proj(0, ATT_W), cos, sa, sb) * scale
    qat_ref[0] = qa.T.astype(BF16)
    ka_ref[...] = _rope(proj(ATT_W, ATT_W), cos, sa, sb).astype(BF16)
    vat_ref[0] = proj(2 * ATT_W, ATT_W).T.astype(BF16)
    c0 = 3 * ATT_W
    qm_ref[...] = proj(c0, ML_QK_W).astype(BF16)
    km = proj(c0 + ML_QK_W, ML_QK_W) * (MLSTM_QK_DIM ** -0.5)
    kmt_ref[0] = km.T.astype(BF16)
    vm_ref[...] = proj(c0 + 2 * ML_QK_W, ML_V_W).astype(BF16)
    mo_ref[...] = proj(c0 + 2 * ML_QK_W + ML_V_W, ML_V_W)

    pre = jnp.dot(h, wg_ref[...], preferred_element_type=F32) + bg_ref[...]
    nh = MLSTM_HEADS
    b = _scan_rows(jax.nn.log_sigmoid(pre), jnp.add, 0.0)
    a = pltpu.roll(pre, nh, 1) - b
    cm = _scan_rows(a, jnp.maximum, NEG_INF)
    lane = lax.broadcasted_iota(jnp.int32, pre.shape, 1)
    gc = jnp.where(lane < nh, pre,
                   jnp.where(lane < 2 * nh, b,
                             jnp.where(lane < 3 * nh, pltpu.roll(a, nh, 1), pltpu.roll(cm, 2 * nh, 1))))
    gc_ref[...] = gc
    gt_ref[0] = gc.T[:GATE_ROWS, :]


def _in_proj(x2d, g, w_main, w_gate, b_gate, cos, sa, sb, batch, seq):
    n = x2d.shape[0]
    tm = ROW_TILE
    nblk = seq // tm
    row = lambda w: pl.BlockSpec((tm, w), lambda i: (i, 0))
    full = lambda a: pl.BlockSpec(a.shape, lambda i: (0,) * a.ndim)
    tr = lambda w: pl.BlockSpec((1, w, tm), lambda i: (i // nblk, 0, i % nblk))
    out_shape = [
        jax.ShapeDtypeStruct((batch, ATT_W, seq), BF16),
        jax.ShapeDtypeStruct((n, ATT_W), BF16),
        jax.ShapeDtypeStruct((batch, ATT_W, seq), BF16),
        jax.ShapeDtypeStruct((n, ML_QK_W), BF16),
        jax.ShapeDtypeStruct((batch, ML_QK_W, seq), BF16),
        jax.ShapeDtypeStruct((n, ML_V_W), BF16),
        jax.ShapeDtypeStruct((n, ML_V_W), F32),
        jax.ShapeDtypeStruct((n, LANES), F32),
        jax.ShapeDtypeStruct((batch, GATE_ROWS, seq), F32),
    ]
    out_specs = [tr(ATT_W), row(ATT_W), tr(ATT_W), row(ML_QK_W), tr(ML_QK_W), row(ML_V_W),
                 row(ML_V_W), row(LANES), tr(GATE_ROWS)]
    return pl.pallas_call(
        _in_proj_kernel,
        grid=(n // tm,),
        in_specs=[row(D_MODEL), full(g), full(w_main), full(w_gate), full(b_gate),
                  row(LANES), row(LANES), row(LANES)],
        out_specs=out_specs,
        out_shape=out_shape,
        compiler_params=pltpu.CompilerParams(dimension_semantics=("arbitrary",),
                                             vmem_limit_bytes=VMEM_LIMIT),
        name="in_proj",
    )(x2d, g, w_main, w_gate, b_gate, cos, sa, sb)


def _moba_kernel(qt_ref, k_ref, vt_ref, o_ref, kmean_ref, bias_ref, acc_ref):
    i = pl.program_id(2)
    mb = MOBA_BLOCK
    nb = kmean_ref.shape[0]

    @pl.when(i == 0)
    def _():
        for jj in range(nb):
            kb = k_ref[jj * mb:(jj + 1) * mb, :].astype(F32)
            kmean_ref[jj:jj + 1, :] = jnp.sum(kb, axis=0, keepdims=True) * (1.0 / mb)

    qt = qt_ref[0].astype(F32)
    row_head = lax.broadcasted_iota(jnp.int32, qt.shape, 0) // ATT_HEAD_DIM
    kmean = kmean_ref[...]
    km_hi = kmean.astype(BF16)
    km_lo = (kmean - km_hi.astype(F32)).astype(BF16)
    blk = lax.broadcasted_iota(jnp.int32, (nb, mb), 0)
    key_pos = lax.broadcasted_iota(jnp.int32, (mb, mb), 0)
    qry_pos = lax.broadcasted_iota(jnp.int32, (mb, mb), 1)

    for h in range(HEADS_PER_GROUP):
        qh = jnp.where(row_head == h, qt, 0.0).astype(BF16)
        gate = (jnp.dot(km_hi, qh, preferred_element_type=F32)
                + jnp.dot(km_lo, qh, preferred_element_type=F32))
        g = jnp.where(blk < i, gate, NEG_INF)
        sel = blk < 0
        for _ in range(MOBA_TOPK):
            mx = jnp.max(g, axis=0, keepdims=True)
            idx = jnp.min(jnp.where(g == mx, blk, nb), axis=0, keepdims=True)
            pick = blk == idx
            sel = sel | pick
            g = jnp.where(pick, NEG_INF, g)
        bias_ref[h] = jnp.where(sel & (blk < i), 0.0, NEG_INF)

        r0 = pl.multiple_of(i * mb, mb)
        s = jnp.dot(k_ref[pl.ds(r0, mb), :], qh, preferred_element_type=F32)
        s = jnp.where(key_pos <= qry_pos, s, NEG_INF)
        m0 = jnp.max(s, axis=0, keepdims=True)
        p = jnp.exp(s - m0)
        l0 = jnp.sum(p, axis=0, keepdims=True)
        vh = vt_ref[0, h * ATT_HEAD_DIM:(h + 1) * ATT_HEAD_DIM, pl.ds(r0, mb)]
        acc0 = jnp.dot(vh, p.astype(BF16), preferred_element_type=F32)

        def body(j, carry):
            m, l, acc = carry
            c0 = pl.multiple_of(j * mb, mb)
            s = jnp.dot(k_ref[pl.ds(c0, mb), :], qh, preferred_element_type=F32)
            s = s + bias_ref[h, pl.ds(j, 1), :]
            m_new = jnp.maximum(m, jnp.max(s, axis=0, keepdims=True))
            alpha = jnp.exp(m - m_new)
            p = jnp.exp(s - m_new)
            l = alpha * l + jnp.sum(p, axis=0, keepdims=True)
            vj = vt_ref[0, h * ATT_HEAD_DIM:(h + 1) * ATT_HEAD_DIM, pl.ds(c0, mb)]
            acc = alpha * acc + jnp.dot(vj, p.astype(BF16), preferred_element_type=F32)
            return m_new, l, acc

        m, l, acc = lax.fori_loop(0, i, body, (m0, l0, acc0))
        acc_ref[h * ATT_HEAD_DIM:(h + 1) * ATT_HEAD_DIM, :] = acc / l

    o_ref[...] = acc_ref[...].T.astype(BF16)


def _moba(qat, ka, vat, batch, seq):
    n = ka.shape[0]
    mb = MOBA_BLOCK
    nb = seq // mb
    groups = ATT_W // GROUP_W
    return pl.pallas_call(
        _moba_kernel,
        grid=(batch, groups, nb),
        in_specs=[
            pl.BlockSpec((1, GROUP_W, mb), lambda b, g, i: (b, g, i)),
            pl.BlockSpec((seq, GROUP_W), lambda b, g, i: (b, g)),
            pl.BlockSpec((1, GROUP_W, seq), lambda b, g, i: (b, g, 0)),
        ],
        out_specs=pl.BlockSpec((mb, GROUP_W), lambda b, g, i: (b * nb + i, g)),
        out_shape=jax.ShapeDtypeStruct((n, ATT_W), BF16),
        scratch_shapes=[
            pltpu.VMEM((nb, GROUP_W), F32),
            pltpu.VMEM((HEADS_PER_GROUP, nb, mb), F32),
            pltpu.VMEM((GROUP_W, mb), F32),
        ],
        compiler_params=pltpu.CompilerParams(
            dimension_semantics=("arbitrary", "arbitrary", "arbitrary"),
            vmem_limit_bytes=VMEM_LIMIT),
        name="moba",
    )(qat, ka, vat)


def _mlstm_kernel(q_ref, kt_ref, v_ref, mo_ref, gc_ref, gt_ref, gout_ref, o_ref, cn_ref, m_ref):
    c = pl.program_id(1)
    L = ROW_TILE
    nh, dk, dv = MLSTM_HEADS, MLSTM_QK_DIM, MLSTM_V_DIM

    @pl.when(c == 0)
    def _():
        cn_ref[...] = jnp.zeros_like(cn_ref)
        m_ref[...] = jnp.zeros_like(m_ref)

    q4 = q_ref[...].astype(F32)
    lane_head = lax.broadcasted_iota(jnp.int32, q4.shape, 1) // dk
    kt4 = kt_ref[0]
    gc = gc_ref[...]
    gt = gt_ref[0]
    t_idx = lax.broadcasted_iota(jnp.int32, (L, L), 0)
    s_idx = lax.broadcasted_iota(jnp.int32, (L, L), 1)
    ones = jnp.ones((L, dv), BF16)
    cn_bf = cn_ref[...].astype(BF16)

    for h in range(nh):
        qh = jnp.where(lane_head == h, q4, 0.0).astype(BF16)
        m_prev = m_ref[h:h + 1, 0:1]
        b_col = gc[:, nh + h:nh + h + 1]
        cm_col = gc[:, 3 * nh + h:3 * nh + h + 1]
        a_row = gt[2 * nh + h:2 * nh + h + 1, :]
        b_last = gt[nh + h:nh + h + 1, L - 1:L]

        big_m = jnp.maximum(cm_col, m_prev)
        qk = jnp.dot(qh, kt4, preferred_element_type=F32)
        e = jnp.exp(jnp.where(s_idx <= t_idx, a_row - big_m, NEG_INF))
        sm = (qk * e).astype(BF16)
        vext = jnp.concatenate([v_ref[:, h * dv:(h + 1) * dv], ones], axis=1)
        intra = jnp.dot(sm, vext, preferred_element_type=F32)
        inter = jnp.dot(qh, cn_bf, preferred_element_type=F32)
        tot = intra + jnp.exp(m_prev - big_m) * inter
        num, den = tot[:, :dv], tot[:, dv:]
        hh = num / jnp.maximum(jnp.abs(den), jnp.exp(-(b_col + big_m)))
        hh = hh * lax.rsqrt(jnp.mean(hh * hh, axis=-1, keepdims=True) + EPS)
        hh = hh * gout_ref[:, h * dv:(h + 1) * dv] * jax.nn.sigmoid(mo_ref[:, h * dv:(h + 1) * dv])
        o_ref[:, h * dv:(h + 1) * dv] = hh.astype(o_ref.dtype)

        m_last = jnp.maximum(m_prev, jnp.max(a_row, axis=1, keepdims=True))
        wg = jnp.exp(a_row - m_last)
        kw = (kt4[h * dk:(h + 1) * dk, :].astype(F32) * wg).astype(BF16)
        upd = jnp.dot(kw, vext, preferred_element_type=F32)
        rows = slice(h * dk, (h + 1) * dk)
        cn_ref[rows, :] = jnp.exp(m_prev - m_last) * cn_ref[rows, :] + upd
        m_ref[h:h + 1, :] = jnp.broadcast_to(b_last + m_last, (1, LANES))


def _mlstm(qm, kmt, vm, mo, gc, gt, g_out, batch, seq):
    n = qm.shape[0]
    L = ROW_TILE
    nc = seq // L
    row = lambda w: pl.BlockSpec((L, w), lambda b, c: (b * nc + c, 0))
    tr = lambda w: pl.BlockSpec((1, w, L), lambda b, c: (b, 0, c))
    return pl.pallas_call(
        _mlstm_kernel,
        grid=(batch, nc),
        in_specs=[row(ML_QK_W), tr(ML_QK_W), row(ML_V_W), row(ML_V_W), row(LANES), tr(GATE_ROWS),
                  pl.BlockSpec((1, ML_V_W), lambda b, c: (0, 0))],
        out_specs=row(ML_V_W),
        out_shape=jax.ShapeDtypeStruct((n, ML_V_W), BF16),
        scratch_shapes=[
            pltpu.VMEM((ML_QK_W, 2 * MLSTM_V_DIM), F32),
            pltpu.VMEM((8, LANES), F32),
        ],
        compiler_params=pltpu.CompilerParams(dimension_semantics=("arbitrary", "arbitrary"),
                                             vmem_limit_bytes=VMEM_LIMIT),
        name="mlstm",
    )(qm, kmt, vm, mo, gc, gt, g_out)


def _out_mlp_kernel(x_ref, att_ref, hm_ref, wo_ref, gpost_ref, gpre_ref, wup_ref, wdn_ref, gpost2_ref,
                    o_ref):
    mix = (jnp.dot(att_ref[...], wo_ref[:ATT_W, :], preferred_element_type=F32)
           + jnp.dot(hm_ref[...], wo_ref[ATT_W:, :], preferred_element_type=F32))
    x1 = x_ref[...] + _rms(mix, gpost_ref[...])
    hn = _rms(x1, gpre_ref[...]).astype(BF16)
    acc = jnp.zeros(x1.shape, F32)
    for c in range(D_FF // FF_CHUNK):
        cols = slice(c * FF_CHUNK, (c + 1) * FF_CHUNK)
        u = jnp.dot(hn, wup_ref[:, cols], preferred_element_type=F32)
        u = jnp.square(jnp.maximum(u, 0.0)).astype(BF16)
        acc = acc + jnp.dot(u, wdn_ref[cols, :], preferred_element_type=F32)
    o_ref[...] = x1 + _rms(acc, gpost2_ref[...])


def _out_mlp(x2d, att, hm, w_out, g_post, g_pre2, w_up, w_down, g_post2):
    n = x2d.shape[0]
    tm = ROW_TILE
    row = lambda w: pl.BlockSpec((tm, w), lambda i: (i, 0))
    full = lambda a: pl.BlockSpec(a.shape, lambda i: (0,) * a.ndim)
    return pl.pallas_call(
        _out_mlp_kernel,
        grid=(n // tm,),
        in_specs=[row(D_MODEL), row(ATT_W), row(ML_V_W), full(w_out), full(g_post), full(g_pre2),
                  full(w_up), full(w_down), full(g_post2)],
        out_specs=row(D_MODEL),
        out_shape=jax.ShapeDtypeStruct((n, D_MODEL), F32),
        compiler_params=pltpu.CompilerParams(dimension_semantics=("arbitrary",),
                                             vmem_limit_bytes=VMEM_LIMIT),
        name="out_mlp",
    )(x2d, att, hm, w_out, g_post, g_pre2, w_up, w_down, g_post2)


def kernel(x, positions, g_mix_pre, w_in, b_igate, b_fgate, g_mlstm_out, w_out, g_mix_post, g_mlp_pre,
           w_up, w_down, g_mlp_post):
    batch, seq, d = x.shape
    assert d == D_MODEL and seq % ROW_TILE == 0 and w_in.shape[-1] == MAIN_W + 2 * MLSTM_HEADS
    depth = w_in.shape[0]
    cos, sa, sb = _rope_tables(positions)
    xs = x.reshape(batch * seq, d)
    pad = LANES - 2 * MLSTM_HEADS
    for layer in range(depth):
        w_main = w_in[layer, :, :MAIN_W].astype(BF16)
        w_gate = jnp.pad(w_in[layer, :, MAIN_W:], ((0, 0), (0, pad))).astype(BF16)
        b_gate = jnp.pad(jnp.concatenate([b_igate[layer], b_fgate[layer]]), (0, pad))[None, :].astype(F32)
        qat, ka, vat, qm, kmt, vm, mo, gc, gt = _in_proj(
            xs, g_mix_pre[layer][None, :], w_main, w_gate, b_gate, cos, sa, sb, batch, seq)
        att = _moba(qat, ka, vat, batch, seq)
        hm = _mlstm(qm, kmt, vm, mo, gc, gt, g_mlstm_out[layer][None, :], batch, seq)
        xs = _out_mlp(xs, att, hm, w_out[layer].astype(BF16), g_mix_post[layer][None, :],
                      g_mlp_pre[layer][None, :], w_up[layer].astype(BF16), w_down[layer].astype(BF16),
                      g_mlp_post[layer][None, :])
    return xs.reshape(batch, seq, d)
```

```python
import functools

import jax
import jax.numpy as jnp
import numpy as np
from jax import lax
from jax.experimental import pallas as pl
from jax.experimental.pallas import tpu as pltpu

F32 = jnp.float32
BF16 = jnp.bfloat16

D_MODEL = 1024
ATT_HEADS = 8
ATT_HEAD_DIM = 64
ATT_W = ATT_HEADS * ATT_HEAD_DIM
ROPE_DIM = ATT_HEAD_DIM // 4
ROPE_THETA = 500000.0
MOBA_BLOCK = 256
MOBA_TOPK = 3
MLSTM_HEADS = 4
MLSTM_QK_DIM = 64
MLSTM_V_DIM = 128
ML_QK_W = MLSTM_HEADS * MLSTM_QK_DIM
ML_V_W = MLSTM_HEADS * MLSTM_V_DIM
D_FF = 4 * D_MODEL
EPS = 1e-6

LANES = 128
ROW_TILE = MOBA_BLOCK
HEADS_PER_GROUP = 4
GROUP_W = HEADS_PER_GROUP * ATT_HEAD_DIM
MAIN_W = 3 * ATT_W + 2 * ML_QK_W + 2 * ML_V_W
GATE_ROWS = 16
FF_CHUNK = 512
VMEM_LIMIT = 56 * 1024 * 1024
NEG_INF = float("-inf")
LOG2E = 1.4426950408889634
BLOCKS_PER_TRIP = 4
SCORE_LOOKAHEAD = 4


def _rms(x, g):
    return x * lax.rsqrt(jnp.mean(x * x, axis=-1, keepdims=True) + EPS) * g


def _rope_tables_kernel(pos_ref, invf_ref, ma_ref, mb_ref, cos_ref, sa_ref, sb_ref):
    ang = pos_ref[...].astype(F32) * invf_ref[...]
    sin = jnp.sin(ang)
    cos_ref[...] = jnp.cos(ang)
    sa_ref[...] = sin * ma_ref[...]
    sb_ref[...] = sin * mb_ref[...]


def _rope_tables(positions):
    n = positions.size
    rows = 1024
    inv_freq = ROPE_THETA ** (-jnp.arange(0, ROPE_DIM, 2, dtype=F32) / ROPE_DIM)
    lane = np.arange(LANES) % ATT_HEAD_DIM
    half = ROPE_DIM // 2
    invf = jnp.where(lane < ROPE_DIM, inv_freq[lane % half], 0.0).astype(F32)[None, :]
    ma = jnp.asarray(np.where(lane < half, -1.0, 0.0), F32)[None, :]
    mb = jnp.asarray(np.where((lane >= half) & (lane < ROPE_DIM), 1.0, 0.0), F32)[None, :]
    tab = jax.ShapeDtypeStruct((n, LANES), F32)
    vec = pl.BlockSpec((1, LANES), lambda i: (0, 0))
    out = pl.BlockSpec((rows, LANES), lambda i: (i, 0))
    return pl.pallas_call(
        _rope_tables_kernel,
        grid=(n // rows,),
        in_specs=[pl.BlockSpec((rows, 1), lambda i: (i, 0)), vec, vec, vec],
        out_specs=[out, out, out],
        out_shape=[tab, tab, tab],
        name="rope_tables",
    )(positions.reshape(n, 1), invf, ma, mb)


def _scan_rows(v, op, ident):
    n = v.shape[0]
    rows = lax.broadcasted_iota(jnp.int32, v.shape, 0)
    s = 1
    while s < n:
        shifted = jnp.where(rows >= s, pltpu.roll(v, s, 0), ident)
        v = op(v, shifted)
        s *= 2
    return v


def _rope(t, cos, sa, sb):
    outs = []
    for c in range(t.shape[1] // LANES):
        tc = t[:, c * LANES:(c + 1) * LANES]
        outs.append(tc * cos + pltpu.roll(tc, LANES - ROPE_DIM // 2, 1) * sa
                    + pltpu.roll(tc, ROPE_DIM // 2, 1) * sb)
    return jnp.concatenate(outs, axis=1)


def _in_proj_kernel(x_ref, g_ref, w_ref, wg_ref, bg_ref, cos_ref, sa_ref, sb_ref,
                    qat_ref, ka_ref, vat_ref, qm_ref, kmt_ref, vm_ref, mo_ref, gc_ref, gt_ref):
    h = _rms(x_ref[...], g_ref[...]).astype(BF16)

    def proj(c0, n):
        return jnp.dot(h, w_ref[:, c0:c0 + n], preferred_element_type=F32)

    cos, sa, sb = cos_ref[...], sa_ref[...], sb_ref[...]
    scale = ATT_HEAD_DIM ** -0.5 * LOG2E
    qa = _rope(proj(0, ATT_W), cos, sa, sb) * scale
    qat_ref[0] = qa.T.astype(BF16)
    ka_ref[...] = _rope(proj(ATT_W, ATT_W), cos, sa, sb).astype(BF16)
    vat_ref[0] = proj(2 * ATT_W, ATT_W).T.astype(BF16)
    c0 = 3 * ATT_W
    qm_ref[...] = proj(c0, ML_QK_W).astype(BF16)
    km = proj(c0 + ML_QK_W, ML_QK_W) * (MLSTM_QK_DIM ** -0.5)
    kmt_ref[0] = km.T.astype(BF16)
    vm_ref[...] = proj(c0 + 2 * ML_QK_W, ML_V_W).astype(BF16)
    mo_ref[...] = proj(c0 + 2 * ML_QK_W + ML_V_W, ML_V_W)

    pre = jnp.dot(h, wg_ref[...], preferred_element_type=F32) + bg_ref[...]
    nh = MLSTM_HEADS
    b = _scan_rows(jax.nn.log_sigmoid(pre), jnp.add, 0.0)
    a = pltpu.roll(pre, nh, 1) - b
    cm = _scan_rows(a, jnp.maximum, NEG_INF)
    lane = lax.broadcasted_iota(jnp.int32, pre.shape, 1)
    gc = jnp.where(lane < nh, pre,
                   jnp.where(lane < 2 * nh, b,
                             jnp.where(lane < 3 * nh, pltpu.roll(a, nh, 1), pltpu.roll(cm, 2 * nh, 1))))
    gc_ref[...] = gc
    gt_ref[0] = gc.T[:GATE_ROWS, :]


def _in_proj(x2d, g, w_main, w_gate, b_gate, cos, sa, sb, batch, seq):
    n = x2d.shape[0]
    tm = ROW_TILE
    nblk = seq // tm
    row = lambda w: pl.BlockSpec((tm, w), lambda i: (i, 0))
    full = lambda a: pl.BlockSpec(a.shape, lambda i: (0,) * a.ndim)
    tr = lambda w: pl.BlockSpec((1, w, tm), lambda i: (i // nblk, 0, i % nblk))
    out_shape = [
        jax.ShapeDtypeStruct((batch, ATT_W, seq), BF16),
        jax.ShapeDtypeStruct((n, ATT_W), BF16),
        jax.ShapeDtypeStruct((batch, ATT_W, seq), BF16),
        jax.ShapeDtypeStruct((n, ML_QK_W), BF16),
        jax.ShapeDtypeStruct((batch, ML_QK_W, seq), BF16),
        jax.ShapeDtypeStruct((n, ML_V_W), BF16),
        jax.ShapeDtypeStruct((n, ML_V_W), F32),
        jax.ShapeDtypeStruct((n, LANES), F32),
        jax.ShapeDtypeStruct((batch, GATE_ROWS, seq), F32),
    ]
    out_specs = [tr(ATT_W), row(ATT_W), tr(ATT_W), row(ML_QK_W), tr(ML_QK_W), row(ML_V_W),
                 row(ML_V_W), row(LANES), tr(GATE_ROWS)]
    return pl.pallas_call(
        _in_proj_kernel,
        grid=(n // tm,),
        in_specs=[row(D_MODEL), full(g), full(w_main), full(w_gate), full(b_gate),
                  row(LANES), row(LANES), row(LANES)],
        out_specs=out_specs,
        out_shape=out_shape,
        compiler_params=pltpu.CompilerParams(dimension_semantics=("arbitrary",),
                                             vmem_limit_bytes=VMEM_LIMIT),
        name="in_proj",
    )(x2d, g, w_main, w_gate, b_gate, cos, sa, sb)


def _moba_kernel(qt_ref, k_ref, vt_ref, o_ref, kmean_ref, qh_ref, sel_ref, acc_ref):
    i = pl.program_id(2)
    mb = MOBA_BLOCK
    nb = kmean_ref.shape[0]
    dh = ATT_HEAD_DIM

    @pl.when(i == 0)
    def _():
        for jj in range(nb):
            kb = k_ref[jj * mb:(jj + 1) * mb, :].astype(F32)
            kmean_ref[jj:jj + 1, :] = jnp.sum(kb, axis=0, keepdims=True) * (1.0 / mb)

    qt = qt_ref[0].astype(F32)
    row_head = lax.broadcasted_iota(jnp.int32, qt.shape, 0) // dh
    kmean = kmean_ref[...]
    km_hi = kmean.astype(BF16)
    km_lo = (kmean - km_hi.astype(F32)).astype(BF16)
    blk = lax.broadcasted_iota(jnp.int32, (nb, mb), 0)
    key_pos = lax.broadcasted_iota(jnp.int32, (mb, mb), 0)
    qry_pos = lax.broadcasted_iota(jnp.int32, (mb, mb), 1)
    r0 = pl.multiple_of(i * mb, mb)
    k_own = k_ref[pl.ds(r0, mb), :]

    m_init, l_init = [], []
    for h in range(HEADS_PER_GROUP):
        rows = slice(h * dh, (h + 1) * dh)
        qh = jnp.where(row_head == h, qt, 0.0).astype(BF16)
        qh_ref[h] = qh
        gate = (jnp.dot(km_hi, qh, preferred_element_type=F32)
                + jnp.dot(km_lo, qh, preferred_element_type=F32))
        g = jnp.where(blk < i, gate, NEG_INF)
        sel = blk < 0
        for _ in range(MOBA_TOPK):
            mx = jnp.max(g, axis=0, keepdims=True)
            idx = jnp.min(jnp.where(g == mx, blk, nb), axis=0, keepdims=True)
            pick = blk == idx
            sel = sel | pick
            g = jnp.where(pick, NEG_INF, g)
        sel_ref[h] = jnp.where(sel & (blk < i), 1.0, 0.0)

        s = jnp.dot(k_own, qh, preferred_element_type=F32)
        s = jnp.where(key_pos <= qry_pos, s, NEG_INF)
        m0 = jnp.max(s, axis=0, keepdims=True)
        p = jnp.exp2(s - m0)
        m_init.append(m0)
        l_init.append(jnp.sum(p, axis=0, keepdims=True))
        acc_ref[rows, :] = jnp.dot(vt_ref[0, rows, pl.ds(r0, mb)], p.astype(BF16),
                                   preferred_element_type=F32)

    def body(t, carry):
        ms, ls = list(carry[0]), list(carry[1])
        blocks = [jnp.minimum(BLOCKS_PER_TRIP * t + u, nb - 1) for u in range(BLOCKS_PER_TRIP)]
        units = [(j, h) for j in blocks for h in range(HEADS_PER_GROUP)]

        def scores(unit):
            j, h = unit
            return jnp.dot(k_ref[pl.ds(pl.multiple_of(j * mb, mb), mb), :], qh_ref[h],
                           preferred_element_type=F32)

        pending = [scores(u) for u in units[:SCORE_LOOKAHEAD]]
        for n, (j, h) in enumerate(units):
            if n + SCORE_LOOKAHEAD < len(units):
                pending.append(scores(units[n + SCORE_LOOKAHEAD]))
            s = pending[n]
            rows = slice(h * dh, (h + 1) * dh)
            picked = sel_ref[h, pl.ds(j, 1), :] > 0.0
            m_new = jnp.where(picked, jnp.maximum(ms[h], jnp.max(s, axis=0, keepdims=True)), ms[h])
            alpha = jnp.exp2(ms[h] - m_new)
            p = jnp.exp2(s - jnp.where(picked, m_new, jnp.inf))
            ms[h] = m_new
            ls[h] = alpha * ls[h] + jnp.sum(p, axis=0, keepdims=True)
            acc_ref[rows, :] = alpha * acc_ref[rows, :] + jnp.dot(
                vt_ref[0, rows, pl.ds(pl.multiple_of(j * mb, mb), mb)], p.astype(BF16),
                preferred_element_type=F32)
        return tuple(ms), tuple(ls)

    trips = (i + BLOCKS_PER_TRIP - 1) // BLOCKS_PER_TRIP
    _, ls = lax.fori_loop(0, trips, body, (tuple(m_init), tuple(l_init)))
    inv_l = jnp.concatenate([jnp.broadcast_to(1.0 / l, (dh, mb)) for l in ls], axis=0)
    o_ref[...] = (acc_ref[...] * inv_l).T.astype(BF16)


def _moba(qat, ka, vat, batch, seq):
    n = ka.shape[0]
    mb = MOBA_BLOCK
    nb = seq // mb
    groups = ATT_W // GROUP_W
    return pl.pallas_call(
        _moba_kernel,
        grid=(batch, groups, nb),
        in_specs=[
            pl.BlockSpec((1, GROUP_W, mb), lambda b, g, i: (b, g, i)),
            pl.BlockSpec((seq, GROUP_W), lambda b, g, i: (b, g)),
            pl.BlockSpec((1, GROUP_W, seq), lambda b, g, i: (b, g, 0)),
        ],
        out_specs=pl.BlockSpec((mb, GROUP_W), lambda b, g, i: (b * nb + i, g)),
        out_shape=jax.ShapeDtypeStruct((n, ATT_W), BF16),
        scratch_shapes=[
            pltpu.VMEM((nb, GROUP_W), F32),
            pltpu.VMEM((HEADS_PER_GROUP, GROUP_W, mb), BF16),
            pltpu.VMEM((HEADS_PER_GROUP, nb, mb), F32),
            pltpu.VMEM((GROUP_W, mb), F32),
        ],
        compiler_params=pltpu.CompilerParams(
            dimension_semantics=("arbitrary", "arbitrary", "arbitrary"),
            vmem_limit_bytes=VMEM_LIMIT),
        name="moba",
    )(qat, ka, vat)


def _mlstm_kernel(q_ref, kt_ref, v_ref, mo_ref, gc_ref, gt_ref, gout_ref, o_ref, cn_ref, m_ref):
    c = pl.program_id(1)
    L = ROW_TILE
    nh, dk, dv = MLSTM_HEADS, MLSTM_QK_DIM, MLSTM_V_DIM

    @pl.when(c == 0)
    def _():
        cn_ref[...] = jnp.zeros_like(cn_ref)
        m_ref[...] = jnp.zeros_like(m_ref)

    q4 = q_ref[...].astype(F32)
    lane_head = lax.broadcasted_iota(jnp.int32, q4.shape, 1) // dk
    kt4 = kt_ref[0]
    gc = gc_ref[...]
    gt = gt_ref[0]
    t_idx = lax.broadcasted_iota(jnp.int32, (L, L), 0)
    s_idx = lax.broadcasted_iota(jnp.int32, (L, L), 1)
    ones = jnp.ones((L, dv), BF16)
    cn_bf = cn_ref[...].astype(BF16)

    for h in range(nh):
        qh = jnp.where(lane_head == h, q4, 0.0).astype(BF16)
        m_prev = m_ref[h:h + 1, 0:1]
        b_col = gc[:, nh + h:nh + h + 1]
        cm_col = gc[:, 3 * nh + h:3 * nh + h + 1]
        a_row = gt[2 * nh + h:2 * nh + h + 1, :]
        b_last = gt[nh + h:nh + h + 1, L - 1:L]

        big_m = jnp.maximum(cm_col, m_prev)
        qk = jnp.dot(qh, kt4, preferred_element_type=F32)
        e = jnp.exp(jnp.where(s_idx <= t_idx, a_row - big_m, NEG_INF))
        sm = (qk * e).astype(BF16)
        vext = jnp.concatenate([v_ref[:, h * dv:(h + 1) * dv], ones], axis=1)
        intra = jnp.dot(sm, vext, preferred_element_type=F32)
        inter = jnp.dot(qh, cn_bf, preferred_element_type=F32)
        tot = intra + jnp.exp(m_prev - big_m) * inter
        num, den = tot[:, :dv], tot[:, dv:]
        hh = num / jnp.maximum(jnp.abs(den), jnp.exp(-(b_col + big_m)))
        hh = hh * lax.rsqrt(jnp.mean(hh * hh, axis=-1, keepdims=True) + EPS)
        hh = hh * gout_ref[:, h * dv:(h + 1) * dv] * jax.nn.sigmoid(mo_ref[:, h * dv:(h + 1) * dv])
        o_ref[:, h * dv:(h + 1) * dv] = hh.astype(o_ref.dtype)

        m_last = jnp.maximum(m_prev, jnp.max(a_row, axis=1, keepdims=True))
        wg = jnp.exp(a_row - m_last)
        kw = (kt4[h * dk:(h + 1) * dk, :].astype(F32) * wg).astype(BF16)
        upd = jnp.dot(kw, vext, preferred_element_type=F32)
        rows = slice(h * dk, (h + 1) * dk)
        cn_ref[rows, :] = jnp.exp(m_prev - m_last) * cn_ref[rows, :] + upd
        m_ref[h:h + 1, :] = jnp.broadcast_to(b_last + m_last, (1, LANES))


def _mlstm(qm, kmt, vm, mo, gc, gt, g_out, batch, seq):
    n = qm.shape[0]
    L = ROW_TILE
    nc = seq // L
    row = lambda w: pl.BlockSpec((L, w), lambda b, c: (b * nc + c, 0))
    tr = lambda w: pl.BlockSpec((1, w, L), lambda b, c: (b, 0, c))
    return pl.pallas_call(
        _mlstm_kernel,
        grid=(batch, nc),
        in_specs=[row(ML_QK_W), tr(ML_QK_W), row(ML_V_W), row(ML_V_W), row(LANES), tr(GATE_ROWS),
                  pl.BlockSpec((1, ML_V_W), lambda b, c: (0, 0))],
        out_specs=row(ML_V_W),
        out_shape=jax.ShapeDtypeStruct((n, ML_V_W), BF16),
        scratch_shapes=[
            pltpu.VMEM((ML_QK_W, 2 * MLSTM_V_DIM), F32),
            pltpu.VMEM((8, LANES), F32),
        ],
        compiler_params=pltpu.CompilerParams(dimension_semantics=("arbitrary", "arbitrary"),
                                             vmem_limit_bytes=VMEM_LIMIT),
        name="mlstm",
    )(qm, kmt, vm, mo, gc, gt, g_out)


def _out_mlp_kernel(x_ref, att_ref, hm_ref, wo_ref, gpost_ref, gpre_ref, wup_ref, wdn_ref, gpost2_ref,
                    o_ref):
    mix = (jnp.dot(att_ref[...], wo_ref[:ATT_W, :], preferred_element_type=F32)
           + jnp.dot(hm_ref[...], wo_ref[ATT_W:, :], preferred_element_type=F32))
    x1 = x_ref[...] + _rms(mix, gpost_ref[...])
    hn = _rms(x1, gpre_ref[...]).astype(BF16)
    acc = jnp.zeros(x1.shape, F32)
    for c in range(D_FF // FF_CHUNK):
        cols = slice(c * FF_CHUNK, (c + 1) * FF_CHUNK)
        u = jnp.dot(hn, wup_ref[:, cols], preferred_element_type=F32)
        u = jnp.square(jnp.maximum(u, 0.0)).astype(BF16)
        acc = acc + jnp.dot(u, wdn_ref[cols, :], preferred_element_type=F32)
    o_ref[...] = x1 + _rms(acc, gpost2_ref[...])


def _out_mlp(x2d, att, hm, w_out, g_post, g_pre2, w_up, w_down, g_post2):
    n = x2d.shape[0]
    tm = ROW_TILE
    row = lambda w: pl.BlockSpec((tm, w), lambda i: (i, 0))
    full = lambda a: pl.BlockSpec(a.shape, lambda i: (0,) * a.ndim)
    return pl.pallas_call(
        _out_mlp_kernel,
        grid=(n // tm,),
        in_specs=[row(D_MODEL), row(ATT_W), row(ML_V_W), full(w_out), full(g_post), full(g_pre2),
                  full(w_up), full(w_down), full(g_post2)],
        out_specs=row(D_MODEL),
        out_shape=jax.ShapeDtypeStruct((n, D_MODEL), F32),
        compiler_params=pltpu.CompilerParams(dimension_semantics=("arbitrary",),
                                             vmem_limit_bytes=VMEM_LIMIT),
        name="out_mlp",
    )(x2d, att, hm, w_out, g_post, g_pre2, w_up, w_down, g_post2)


def kernel(x, positions, g_mix_pre, w_in, b_igate, b_fgate, g_mlstm_out, w_out, g_mix_post, g_mlp_pre,
           w_up, w_down, g_mlp_post):
    batch, seq, d = x.shape
    assert d == D_MODEL and seq % ROW_TILE == 0 and w_in.shape[-1] == MAIN_W + 2 * MLSTM_HEADS
    depth = w_in.shape[0]
    cos, sa, sb = _rope_tables(positions)
    xs = x.reshape(batch * seq, d)
    pad = LANES - 2 * MLSTM_HEADS
    for layer in range(depth):
        w_main = w_in[layer, :, :MAIN_W].astype(BF16)
        w_gate = jnp.pad(w_in[layer, :, MAIN_W:], ((0, 0), (0, pad))).astype(BF16)
        b_gate = jnp.pad(jnp.concatenate([b_igate[layer], b_fgate[layer]]), (0, pad))[None, :].astype(F32)
        qat, ka, vat, qm, kmt, vm, mo, gc, gt = _in_proj(
            xs, g_mix_pre[layer][None, :], w_main, w_gate, b_gate, cos, sa, sb, batch, seq)
        att = _moba(qat, ka, vat, batch, seq)
        hm = _mlstm(qm, kmt, vm, mo, gc, gt, g_mlstm_out[layer][None, :], batch, seq)
        xs = _out_mlp(xs, att, hm, w_out[layer].astype(BF16), g_mix_post[layer][None, :],
                      g_mlp_pre[layer][None, :], w_up[layer].astype(BF16), w_down[layer].astype(BF16),
                      g_mlp_post[layer][None, :])
    return xs.reshape(batch, seq, d)
```

```python
import functools

import jax
import jax.numpy as jnp
import numpy as np
from jax import lax
from jax.experimental import pallas as pl
from jax.experimental.pallas import tpu as pltpu

F32 = jnp.float32
BF16 = jnp.bfloat16

D_MODEL = 1024
ATT_HEADS = 8
ATT_HEAD_DIM = 64
ATT_W = ATT_HEADS * ATT_HEAD_DIM
ROPE_DIM = ATT_HEAD_DIM // 4
ROPE_THETA = 500000.0
MOBA_BLOCK = 256
MOBA_TOPK = 3
MLSTM_HEADS = 4
MLSTM_QK_DIM = 64
MLSTM_V_DIM = 128
ML_QK_W = MLSTM_HEADS * MLSTM_QK_DIM
ML_V_W = MLSTM_HEADS * MLSTM_V_DIM
D_FF = 4 * D_MODEL
EPS = 1e-6

LANES = 128
ROW_TILE = MOBA_BLOCK
MLP_ROW_TILE = 512
HEADS_PER_GROUP = 4
GROUP_W = HEADS_PER_GROUP * ATT_HEAD_DIM
MAIN_W = 3 * ATT_W + 2 * ML_QK_W + 2 * ML_V_W
GATE_ROWS = 16
FF_CHUNK = 512
VMEM_LIMIT = 56 * 1024 * 1024
NEG_INF = float("-inf")
LOG2E = 1.4426950408889634
BLOCKS_PER_TRIP = 4
SCORE_LOOKAHEAD = HEADS_PER_GROUP
V_ROWS = ATT_HEAD_DIM + 16


def _rms(x, g):
    return x * lax.rsqrt(jnp.mean(x * x, axis=-1, keepdims=True) + EPS) * g


def _rope_tables_kernel(pos_ref, invf_ref, ma_ref, mb_ref, cos_ref, sa_ref, sb_ref):
    ang = pos_ref[...].astype(F32) * invf_ref[...]
    sin = jnp.sin(ang)
    cos_ref[...] = jnp.cos(ang)
    sa_ref[...] = sin * ma_ref[...]
    sb_ref[...] = sin * mb_ref[...]


def _rope_tables(positions):
    n = positions.size
    rows = 1024
    inv_freq = ROPE_THETA ** (-jnp.arange(0, ROPE_DIM, 2, dtype=F32) / ROPE_DIM)
    lane = np.arange(LANES) % ATT_HEAD_DIM
    half = ROPE_DIM // 2
    invf = jnp.where(lane < ROPE_DIM, inv_freq[lane % half], 0.0).astype(F32)[None, :]
    ma = jnp.asarray(np.where(lane < half, -1.0, 0.0), F32)[None, :]
    mb = jnp.asarray(np.where((lane >= half) & (lane < ROPE_DIM), 1.0, 0.0), F32)[None, :]
    tab = jax.ShapeDtypeStruct((n, LANES), F32)
    vec = pl.BlockSpec((1, LANES), lambda i: (0, 0))
    out = pl.BlockSpec((rows, LANES), lambda i: (i, 0))
    return pl.pallas_call(
        _rope_tables_kernel,
        grid=(n // rows,),
        in_specs=[pl.BlockSpec((rows, 1), lambda i: (i, 0)), vec, vec, vec],
        out_specs=[out, out, out],
        out_shape=[tab, tab, tab],
        name="rope_tables",
    )(positions.reshape(n, 1), invf, ma, mb)


def _scan_rows(v, op, ident):
    n = v.shape[0]
    rows = lax.broadcasted_iota(jnp.int32, v.shape, 0)
    s = 1
    while s < n:
        shifted = jnp.where(rows >= s, pltpu.roll(v, s, 0), ident)
        v = op(v, shifted)
        s *= 2
    return v


def _rope(t, cos, sa, sb):
    outs = []
    for c in range(t.shape[1] // LANES):
        tc = t[:, c * LANES:(c + 1) * LANES]
        outs.append(tc * cos + pltpu.roll(tc, LANES - ROPE_DIM // 2, 1) * sa
                    + pltpu.roll(tc, ROPE_DIM // 2, 1) * sb)
    return jnp.concatenate(outs, axis=1)


def _in_proj_kernel(x_ref, g_ref, w_ref, wg_ref, bg_ref, cos_ref, sa_ref, sb_ref,
                    qat_ref, ka_ref, vat_ref, qm_ref, kmt_ref, vm_ref, mo_ref, gc_ref, gt_ref):
    h = _rms(x_ref[...], g_ref[...]).astype(BF16)

    def proj(c0, n):
        return jnp.dot(h, w_ref[:, c0:c0 + n], preferred_element_type=F32)

    cos, sa, sb = cos_ref[...], sa_ref[...], sb_ref[...]
    scale = ATT_HEAD_DIM ** -0.5 * LOG2E
    qa = _rope(proj(0, ATT_W), cos, sa, sb) * scale
    qat_ref[0] = qa.T.astype(BF16)
    ka_ref[...] = _rope(proj(ATT_W, ATT_W), cos, sa, sb).astype(BF16)
    vat = proj(2 * ATT_W, ATT_W).T.astype(BF16)
    ones = jnp.ones((V_ROWS - ATT_HEAD_DIM, vat.shape[1]), BF16)
    for hd in range(ATT_HEADS):
        vat_ref[0, hd * V_ROWS:hd * V_ROWS + ATT_HEAD_DIM, :] = vat[hd * ATT_HEAD_DIM:(hd + 1) * ATT_HEAD_DIM]
        vat_ref[0, hd * V_ROWS + ATT_HEAD_DIM:(hd + 1) * V_ROWS, :] = ones
    c0 = 3 * ATT_W
    qm_ref[...] = proj(c0, ML_QK_W).astype(BF16)
    km = proj(c0 + ML_QK_W, ML_QK_W) * (MLSTM_QK_DIM ** -0.5)
    kmt_ref[0] = km.T.astype(BF16)
    vm_ref[...] = proj(c0 + 2 * ML_QK_W, ML_V_W).astype(BF16)
    mo_ref[...] = proj(c0 + 2 * ML_QK_W + ML_V_W, ML_V_W)

    pre = jnp.dot(h, wg_ref[...], preferred_element_type=F32) + bg_ref[...]
    nh = MLSTM_HEADS
    b = _scan_rows(jax.nn.log_sigmoid(pre), jnp.add, 0.0)
    a = pltpu.roll(pre, nh, 1) - b
    cm = _scan_rows(a, jnp.maximum, NEG_INF)
    lane = lax.broadcasted_iota(jnp.int32, pre.shape, 1)
    gc = jnp.where(lane < nh, pre,
                   jnp.where(lane < 2 * nh, b,
                             jnp.where(lane < 3 * nh, pltpu.roll(a, nh, 1), pltpu.roll(cm, 2 * nh, 1))))
    gc_ref[...] = gc
    gt_ref[0] = gc.T[:GATE_ROWS, :]


def _in_proj(x2d, g, w_main, w_gate, b_gate, cos, sa, sb, batch, seq):
    n = x2d.shape[0]
    tm = ROW_TILE
    nblk = seq // tm
    row = lambda w: pl.BlockSpec((tm, w), lambda i: (i, 0))
    full = lambda a: pl.BlockSpec(a.shape, lambda i: (0,) * a.ndim)
    tr = lambda w: pl.BlockSpec((1, w, tm), lambda i: (i // nblk, 0, i % nblk))
    out_shape = [
        jax.ShapeDtypeStruct((batch, ATT_W, seq), BF16),
        jax.ShapeDtypeStruct((n, ATT_W), BF16),
        jax.ShapeDtypeStruct((batch, ATT_HEADS * V_ROWS, seq), BF16),
        jax.ShapeDtypeStruct((n, ML_QK_W), BF16),
        jax.ShapeDtypeStruct((batch, ML_QK_W, seq), BF16),
        jax.ShapeDtypeStruct((n, ML_V_W), BF16),
        jax.ShapeDtypeStruct((n, ML_V_W), F32),
        jax.ShapeDtypeStruct((n, LANES), F32),
        jax.ShapeDtypeStruct((batch, GATE_ROWS, seq), F32),
    ]
    out_specs = [tr(ATT_W), row(ATT_W), tr(ATT_HEADS * V_ROWS), row(ML_QK_W), tr(ML_QK_W), row(ML_V_W),
                 row(ML_V_W), row(LANES), tr(GATE_ROWS)]
    return pl.pallas_call(
        _in_proj_kernel,
        grid=(n // tm,),
        in_specs=[row(D_MODEL), full(g), full(w_main), full(w_gate), full(b_gate),
                  row(LANES), row(LANES), row(LANES)],
        out_specs=out_specs,
        out_shape=out_shape,
        compiler_params=pltpu.CompilerParams(dimension_semantics=("arbitrary",),
                                             vmem_limit_bytes=VMEM_LIMIT),
        name="in_proj",
    )(x2d, g, w_main, w_gate, b_gate, cos, sa, sb)


def _moba_kernel(qt_ref, k_ref, vt_ref, o_ref, kmean_ref, qh_ref, sel_ref, acc_ref, s_ref):
    i = pl.program_id(2)
    mb = MOBA_BLOCK
    nb = kmean_ref.shape[0]
    dh = ATT_HEAD_DIM

    @pl.when(i == 0)
    def _():
        for jj in range(nb):
            kb = k_ref[jj * mb:(jj + 1) * mb, :].astype(F32)
            kmean_ref[jj:jj + 1, :] = jnp.sum(kb, axis=0, keepdims=True) * (1.0 / mb)

    qt = qt_ref[0].astype(F32)
    row_head = lax.broadcasted_iota(jnp.int32, qt.shape, 0) // dh
    kmean = kmean_ref[...]
    km_hi = kmean.astype(BF16)
    km_lo = (kmean - km_hi.astype(F32)).astype(BF16)
    blk = lax.broadcasted_iota(jnp.int32, (nb, mb), 0)
    key_pos = lax.broadcasted_iota(jnp.int32, (mb, mb), 0)
    qry_pos = lax.broadcasted_iota(jnp.int32, (mb, mb), 1)
    r0 = pl.multiple_of(i * mb, mb)
    k_own = k_ref[pl.ds(r0, mb), :]

    heads = range(HEADS_PER_GROUP)
    units_per_trip = BLOCKS_PER_TRIP * HEADS_PER_GROUP

    def scores(j, h):
        return jnp.dot(k_ref[pl.ds(pl.multiple_of(j * mb, mb), mb), :], qh_ref[h],
                       preferred_element_type=F32)

    def v_rows(h):
        return slice(h * V_ROWS, (h + 1) * V_ROWS)

    for h in heads:
        qh_ref[h] = jnp.where(row_head == h, qt, 0.0).astype(BF16)
    s_own = [jnp.dot(k_own, qh_ref[h], preferred_element_type=F32) for h in heads]
    gates = [jnp.dot(km_hi, qh_ref[h], preferred_element_type=F32)
             + jnp.dot(km_lo, qh_ref[h], preferred_element_type=F32) for h in heads]
    lead = [scores(0, h) for h in heads]

    m_init = []
    for h in heads:
        g = jnp.where(blk < i, gates[h], NEG_INF)
        sel = blk < 0
        for _ in range(MOBA_TOPK):
            mx = jnp.max(g, axis=0, keepdims=True)
            idx = jnp.min(jnp.where(g == mx, blk, nb), axis=0, keepdims=True)
            pick = blk == idx
            sel = sel | pick
            g = jnp.where(pick, NEG_INF, g)
        sel_ref[h] = jnp.where(sel & (blk < i), 1.0, 0.0)

        s = jnp.where(key_pos <= qry_pos, s_own[h], NEG_INF)
        m0 = jnp.max(s, axis=0, keepdims=True)
        m_init.append(m0)
        acc_ref[v_rows(h), :] = jnp.dot(vt_ref[0, v_rows(h), pl.ds(r0, mb)],
                                        jnp.exp2(s - m0).astype(BF16), preferred_element_type=F32)
    for h in heads:
        s_ref[h] = lead[h]

    def body(t, ms):
        ms = list(ms)
        blocks = [jnp.minimum(BLOCKS_PER_TRIP * t + u, nb - 1) for u in range(BLOCKS_PER_TRIP + 1)]
        units = [(j, h) for j in blocks for h in heads]
        pending = {n: s_ref[n] for n in range(SCORE_LOOKAHEAD)}
        for n in range(units_per_trip):
            pending[n + SCORE_LOOKAHEAD] = scores(*units[n + SCORE_LOOKAHEAD])
            j, h = units[n]
            s = pending.pop(n)
            picked = sel_ref[h, pl.ds(j, 1), :] > 0.0
            m_new = jnp.where(picked, jnp.maximum(ms[h], jnp.max(s, axis=0, keepdims=True)), ms[h])
            alpha = jnp.exp2(ms[h] - m_new)
            p = jnp.exp2(s - jnp.where(picked, m_new, jnp.inf))
            ms[h] = m_new
            acc_ref[v_rows(h), :] = alpha * acc_ref[v_rows(h), :] + jnp.dot(
                vt_ref[0, v_rows(h), pl.ds(pl.multiple_of(j * mb, mb), mb)], p.astype(BF16),
                preferred_element_type=F32)
        for n in range(SCORE_LOOKAHEAD):
            s_ref[n] = pending[units_per_trip + n]
        return tuple(ms)

    trips = (i + BLOCKS_PER_TRIP - 1) // BLOCKS_PER_TRIP
    lax.fori_loop(0, trips, body, tuple(m_init))
    outs = []
    for h in heads:
        a = acc_ref[v_rows(h), :]
        outs.append(a[:dh] * (1.0 / a[dh:dh + 1]))
    o_ref[...] = jnp.concatenate(outs, axis=0).T.astype(BF16)


def _moba(qat, ka, vat, batch, seq):
    n = ka.shape[0]
    mb = MOBA_BLOCK
    nb = seq // mb
    groups = ATT_W // GROUP_W
    return pl.pallas_call(
        _moba_kernel,
        grid=(batch, groups, nb),
        in_specs=[
            pl.BlockSpec((1, GROUP_W, mb), lambda b, g, i: (b, g, i)),
            pl.BlockSpec((seq, GROUP_W), lambda b, g, i: (b, g)),
            pl.BlockSpec((1, HEADS_PER_GROUP * V_ROWS, seq), lambda b, g, i: (b, g, 0)),
        ],
        out_specs=pl.BlockSpec((mb, GROUP_W), lambda b, g, i: (b * nb + i, g)),
        out_shape=jax.ShapeDtypeStruct((n, ATT_W), BF16),
        scratch_shapes=[
            pltpu.VMEM((nb, GROUP_W), F32),
            pltpu.VMEM((HEADS_PER_GROUP, GROUP_W, mb), BF16),
            pltpu.VMEM((HEADS_PER_GROUP, nb, mb), F32),
            pltpu.VMEM((HEADS_PER_GROUP * V_ROWS, mb), F32),
            pltpu.VMEM((SCORE_LOOKAHEAD, mb, mb), F32),
        ],
        compiler_params=pltpu.CompilerParams(
            dimension_semantics=("arbitrary", "arbitrary", "arbitrary"),
            vmem_limit_bytes=VMEM_LIMIT),
        name="moba",
    )(qat, ka, vat)


def _mlstm_kernel(q_ref, kt_ref, v_ref, mo_ref, gc_ref, gt_ref, gout_ref, o_ref, cn_ref, m_ref):
    c = pl.program_id(1)
    L = ROW_TILE
    nh, dk, dv = MLSTM_HEADS, MLSTM_QK_DIM, MLSTM_V_DIM

    @pl.when(c == 0)
    def _():
        cn_ref[...] = jnp.zeros_like(cn_ref)
        m_ref[...] = jnp.zeros_like(m_ref)

    q4 = q_ref[...].astype(F32)
    lane_head = lax.broadcasted_iota(jnp.int32, q4.shape, 1) // dk
    kt4 = kt_ref[0]
    gc = gc_ref[...]
    gt = gt_ref[0]
    t_idx = lax.broadcasted_iota(jnp.int32, (L, L), 0)
    s_idx = lax.broadcasted_iota(jnp.int32, (L, L), 1)
    ones = jnp.ones((L, dv), BF16)
    cn_bf = cn_ref[...].astype(BF16)

    heads = range(nh)
    qhs = [jnp.where(lane_head == h, q4, 0.0).astype(BF16) for h in heads]
    qks = [jnp.dot(qhs[h], kt4, preferred_element_type=F32) for h in heads]
    inters = [jnp.dot(qhs[h], cn_bf, preferred_element_type=F32) for h in heads]
    vexts = [jnp.concatenate([v_ref[:, h * dv:(h + 1) * dv], ones], axis=1) for h in heads]
    m_prevs = [m_ref[h:h + 1, 0:1] for h in heads]
    a_rows = [gt[2 * nh + h:2 * nh + h + 1, :] for h in heads]

    for h in heads:
        m_last = jnp.maximum(m_prevs[h], jnp.max(a_rows[h], axis=1, keepdims=True))
        wg = jnp.exp(a_rows[h] - m_last)
        kw = (kt4[h * dk:(h + 1) * dk, :].astype(F32) * wg).astype(BF16)
        upd = jnp.dot(kw, vexts[h], preferred_element_type=F32)
        rows = slice(h * dk, (h + 1) * dk)
        cn_ref[rows, :] = jnp.exp(m_prevs[h] - m_last) * cn_ref[rows, :] + upd
        b_last = gt[nh + h:nh + h + 1, L - 1:L]
        m_ref[h:h + 1, :] = jnp.broadcast_to(b_last + m_last, (1, LANES))

    lane = lax.broadcasted_iota(jnp.int32, (1, LANES), 1)
    m_lanes = jnp.zeros((1, LANES), F32)
    for h in heads:
        m_lanes = jnp.where(lane == 3 * nh + h, m_prevs[h], m_lanes)
    big_all = jnp.maximum(gc, m_lanes)
    mt_all = pltpu.roll(gc, 2 * nh, 1) + big_all

    big_bs, intras = [], []
    for h in heads:
        col = slice(3 * nh + h, 3 * nh + h + 1)
        big_b = jnp.broadcast_to(big_all[:, col], (L, dv))
        big_b2 = jnp.concatenate([big_b, big_b], axis=1)
        e = jnp.exp(jnp.where(s_idx <= t_idx, a_rows[h] - big_b2, NEG_INF))
        sm = (qks[h] * e).astype(BF16)
        big_bs.append(big_b2)
        intras.append(jnp.dot(sm, vexts[h], preferred_element_type=F32))

    for h in heads:
        col = slice(3 * nh + h, 3 * nh + h + 1)
        tot = intras[h] + jnp.exp(m_prevs[h] - big_bs[h]) * inters[h]
        num, den = tot[:, :dv], tot[:, dv:]
        hh = num / jnp.maximum(jnp.abs(den), jnp.exp(-jnp.broadcast_to(mt_all[:, col], (L, dv))))
        hh = hh * lax.rsqrt(jnp.mean(hh * hh, axis=-1, keepdims=True) + EPS)
        hh = hh * gout_ref[:, h * dv:(h + 1) * dv] * jax.nn.sigmoid(mo_ref[:, h * dv:(h + 1) * dv])
        o_ref[:, h * dv:(h + 1) * dv] = hh.astype(o_ref.dtype)


def _mlstm(qm, kmt, vm, mo, gc, gt, g_out, batch, seq):
    n = qm.shape[0]
    L = ROW_TILE
    nc = seq // L
    row = lambda w: pl.BlockSpec((L, w), lambda b, c: (b * nc + c, 0))
    tr = lambda w: pl.BlockSpec((1, w, L), lambda b, c: (b, 0, c))
    return pl.pallas_call(
        _mlstm_kernel,
        grid=(batch, nc),
        in_specs=[row(ML_QK_W), tr(ML_QK_W), row(ML_V_W), row(ML_V_W), row(LANES), tr(GATE_ROWS),
                  pl.BlockSpec((1, ML_V_W), lambda b, c: (0, 0))],
        out_specs=row(ML_V_W),
        out_shape=jax.ShapeDtypeStruct((n, ML_V_W), BF16),
        scratch_shapes=[
            pltpu.VMEM((ML_QK_W, 2 * MLSTM_V_DIM), F32),
            pltpu.VMEM((8, LANES), F32),
        ],
        compiler_params=pltpu.CompilerParams(dimension_semantics=("arbitrary", "arbitrary"),
                                             vmem_limit_bytes=VMEM_LIMIT),
        name="mlstm",
    )(qm, kmt, vm, mo, gc, gt, g_out)


def _out_mlp_kernel(x_ref, att_ref, hm_ref, wo_ref, gpost_ref, gpre_ref, wup_ref, wdn_ref, gpost2_ref,
                    o_ref):
    mix = (jnp.dot(att_ref[...], wo_ref[:ATT_W, :], preferred_element_type=F32)
           + jnp.dot(hm_ref[...], wo_ref[ATT_W:, :], preferred_element_type=F32))
    x1 = x_ref[...] + _rms(mix, gpost_ref[...])
    hn = _rms(x1, gpre_ref[...]).astype(BF16)
    acc = jnp.zeros(x1.shape, F32)
    for c in range(D_FF // FF_CHUNK):
        cols = slice(c * FF_CHUNK, (c + 1) * FF_CHUNK)
        u = jnp.dot(hn, wup_ref[:, cols], preferred_element_type=F32)
        u = jnp.square(jnp.maximum(u, 0.0)).astype(BF16)
        acc = acc + jnp.dot(u, wdn_ref[cols, :], preferred_element_type=F32)
    o_ref[...] = x1 + _rms(acc, gpost2_ref[...])


def _out_mlp(x2d, att, hm, w_out, g_post, g_pre2, w_up, w_down, g_post2):
    n = x2d.shape[0]
    tm = MLP_ROW_TILE
    row = lambda w: pl.BlockSpec((tm, w), lambda i: (i, 0))
    full = lambda a: pl.BlockSpec(a.shape, lambda i: (0,) * a.ndim)
    return pl.pallas_call(
        _out_mlp_kernel,
        grid=(n // tm,),
        in_specs=[row(D_MODEL), row(ATT_W), row(ML_V_W), full(w_out), full(g_post), full(g_pre2),
                  full(w_up), full(w_down), full(g_post2)],
        out_specs=row(D_MODEL),
        out_shape=jax.ShapeDtypeStruct((n, D_MODEL), F32),
        compiler_params=pltpu.CompilerParams(dimension_semantics=("arbitrary",),
                                             vmem_limit_bytes=VMEM_LIMIT),
        name="out_mlp",
    )(x2d, att, hm, w_out, g_post, g_pre2, w_up, w_down, g_post2)


def kernel(x, positions, g_mix_pre, w_in, b_igate, b_fgate, g_mlstm_out, w_out, g_mix_post, g_mlp_pre,
           w_up, w_down, g_mlp_post):
    batch, seq, d = x.shape
    assert d == D_MODEL and seq % ROW_TILE == 0 and w_in.shape[-1] == MAIN_W + 2 * MLSTM_HEADS
    depth = w_in.shape[0]
    cos, sa, sb = _rope_tables(positions)
    xs = x.reshape(batch * seq, d)
    pad = LANES - 2 * MLSTM_HEADS
    for layer in range(depth):
        w_main = w_in[layer, :, :MAIN_W].astype(BF16)
        w_gate = jnp.pad(w_in[layer, :, MAIN_W:], ((0, 0), (0, pad))).astype(BF16)
        b_gate = jnp.pad(jnp.concatenate([b_igate[layer], b_fgate[layer]]), (0, pad))[None, :].astype(F32)
        qat, ka, vat, qm, kmt, vm, mo, gc, gt = _in_proj(
            xs, g_mix_pre[layer][None, :], w_main, w_gate, b_gate, cos, sa, sb, batch, seq)
        att = _moba(qat, ka, vat, batch, seq)
        hm = _mlstm(qm, kmt, vm, mo, gc, gt, g_mlstm_out[layer][None, :], batch, seq)
        xs = _out_mlp(xs, att, hm, w_out[layer].astype(BF16), g_mix_post[layer][None, :],
                      g_mlp_pre[layer][None, :], w_up[layer].astype(BF16), w_down[layer].astype(BF16),
                      g_mlp_post[layer][None, :])
    return xs.reshape(batch, seq, d)
```

```python
import functools

import jax
import jax.numpy as jnp
import numpy as np
from jax import lax
from jax.experimental import pallas as pl
from jax.experimental.pallas import tpu as pltpu

F32 = jnp.float32
BF16 = jnp.bfloat16

D_MODEL = 1024
ATT_HEADS = 8
ATT_HEAD_DIM = 64
ATT_W = ATT_HEADS * ATT_HEAD_DIM
ROPE_DIM = ATT_HEAD_DIM // 4
ROPE_THETA = 500000.0
MOBA_BLOCK = 256
MOBA_TOPK = 3
MLSTM_HEADS = 4
MLSTM_QK_DIM = 64
MLSTM_V_DIM = 128
ML_QK_W = MLSTM_HEADS * MLSTM_QK_DIM
ML_V_W = MLSTM_HEADS * MLSTM_V_DIM
D_FF = 4 * D_MODEL
EPS = 1e-6

LANES = 128
ROW_TILE = MOBA_BLOCK
IN_ROW_TILE = 512
MLP_ROW_TILE = 512
HEADS_PER_GROUP = 4
GROUP_W = HEADS_PER_GROUP * ATT_HEAD_DIM
MAIN_W = 3 * ATT_W + 2 * ML_QK_W + 2 * ML_V_W
GATE_ROWS = 16
FF_CHUNK = 512
VMEM_LIMIT = 56 * 1024 * 1024
NEG_INF = float("-inf")
LOG2E = 1.4426950408889634
BLOCKS_PER_TRIP = 4
SCORE_LOOKAHEAD = HEADS_PER_GROUP
NEXT_TRIP_ISSUE_AT = 12
V_ROWS = ATT_HEAD_DIM + 16


def _rms(x, g):
    return x * lax.rsqrt(jnp.mean(x * x, axis=-1, keepdims=True) + EPS) * g


def _rope_tables_kernel(pos_ref, invf_ref, ma_ref, mb_ref, cos_ref, sa_ref, sb_ref):
    ang = pos_ref[...].astype(F32) * invf_ref[...]
    sin = jnp.sin(ang)
    cos_ref[...] = jnp.cos(ang)
    sa_ref[...] = sin * ma_ref[...]
    sb_ref[...] = sin * mb_ref[...]


def _rope_tables(positions):
    n = positions.size
    rows = 1024
    inv_freq = ROPE_THETA ** (-jnp.arange(0, ROPE_DIM, 2, dtype=F32) / ROPE_DIM)
    lane = np.arange(LANES) % ATT_HEAD_DIM
    half = ROPE_DIM // 2
    invf = jnp.where(lane < ROPE_DIM, inv_freq[lane % half], 0.0).astype(F32)[None, :]
    ma = jnp.asarray(np.where(lane < half, -1.0, 0.0), F32)[None, :]
    mb = jnp.asarray(np.where((lane >= half) & (lane < ROPE_DIM), 1.0, 0.0), F32)[None, :]
    tab = jax.ShapeDtypeStruct((n, LANES), F32)
    vec = pl.BlockSpec((1, LANES), lambda i: (0, 0))
    out = pl.BlockSpec((rows, LANES), lambda i: (i, 0))
    return pl.pallas_call(
        _rope_tables_kernel,
        grid=(n // rows,),
        in_specs=[pl.BlockSpec((rows, 1), lambda i: (i, 0)), vec, vec, vec],
        out_specs=[out, out, out],
        out_shape=[tab, tab, tab],
        name="rope_tables",
    )(positions.reshape(n, 1), invf, ma, mb)


def _scan_rows(v, op, ident):
    n = v.shape[0]
    rows = lax.broadcasted_iota(jnp.int32, v.shape, 0)
    s = 1
    while s < n:
        shifted = jnp.where(rows >= s, pltpu.roll(v, s, 0), ident)
        v = op(v, shifted)
        s *= 2
    return v


def _rope(t, cos, sa, sb):
    outs = []
    for c in range(t.shape[1] // LANES):
        tc = t[:, c * LANES:(c + 1) * LANES]
        outs.append(tc * cos + pltpu.roll(tc, LANES - ROPE_DIM // 2, 1) * sa
                    + pltpu.roll(tc, ROPE_DIM // 2, 1) * sb)
    return jnp.concatenate(outs, axis=1)


def _in_proj_kernel(x_ref, g_ref, w_ref, wg_ref, bg_ref, cos_ref, sa_ref, sb_ref,
                    qat_ref, ka_ref, vat_ref, qm_ref, kmt_ref, vm_ref, mo_ref, gc_ref, gt_ref):
    h = _rms(x_ref[...], g_ref[...]).astype(BF16)

    def proj(c0, n):
        return jnp.dot(h, w_ref[:, c0:c0 + n], preferred_element_type=F32)

    cos, sa, sb = cos_ref[...], sa_ref[...], sb_ref[...]
    scale = ATT_HEAD_DIM ** -0.5 * LOG2E
    qa = _rope(proj(0, ATT_W), cos, sa, sb) * scale
    qat_ref[0] = qa.T.astype(BF16)
    ka_ref[...] = _rope(proj(ATT_W, ATT_W), cos, sa, sb).astype(BF16)
    vat = proj(2 * ATT_W, ATT_W).T.astype(BF16)
    ones = jnp.ones((V_ROWS - ATT_HEAD_DIM, vat.shape[1]), BF16)
    for hd in range(ATT_HEADS):
        vat_ref[0, hd * V_ROWS:hd * V_ROWS + ATT_HEAD_DIM, :] = vat[hd * ATT_HEAD_DIM:(hd + 1) * ATT_HEAD_DIM]
        vat_ref[0, hd * V_ROWS + ATT_HEAD_DIM:(hd + 1) * V_ROWS, :] = ones
    c0 = 3 * ATT_W
    qm_ref[...] = proj(c0, ML_QK_W).astype(BF16)
    km = proj(c0 + ML_QK_W, ML_QK_W) * (MLSTM_QK_DIM ** -0.5)
    kmt_ref[0] = km.T.astype(BF16)
    vm_ref[...] = proj(c0 + 2 * ML_QK_W, ML_V_W).astype(BF16)
    mo_ref[...] = proj(c0 + 2 * ML_QK_W + ML_V_W, ML_V_W)

    pre_all = jnp.dot(h, wg_ref[...], preferred_element_type=F32) + bg_ref[...]
    nh = MLSTM_HEADS
    L = ROW_TILE
    for c in range(pre_all.shape[0] // L):
        pre = pre_all[c * L:(c + 1) * L]
        b = _scan_rows(jax.nn.log_sigmoid(pre), jnp.add, 0.0)
        a = pltpu.roll(pre, nh, 1) - b
        cm = _scan_rows(a, jnp.maximum, NEG_INF)
        lane = lax.broadcasted_iota(jnp.int32, pre.shape, 1)
        gc = jnp.where(lane < nh, pre,
                       jnp.where(lane < 2 * nh, b,
                                 jnp.where(lane < 3 * nh, pltpu.roll(a, nh, 1), pltpu.roll(cm, 2 * nh, 1))))
        gc_ref[c * L:(c + 1) * L, :] = gc
        gt_ref[0, :, c * L:(c + 1) * L] = gc.T[:GATE_ROWS, :]


def _in_proj(x2d, g, w_main, w_gate, b_gate, cos, sa, sb, batch, seq):
    n = x2d.shape[0]
    tm = IN_ROW_TILE
    nblk = seq // tm
    row = lambda w: pl.BlockSpec((tm, w), lambda i: (i, 0))
    full = lambda a: pl.BlockSpec(a.shape, lambda i: (0,) * a.ndim)
    tr = lambda w: pl.BlockSpec((1, w, tm), lambda i: (i // nblk, 0, i % nblk))
    out_shape = [
        jax.ShapeDtypeStruct((batch, ATT_W, seq), BF16),
        jax.ShapeDtypeStruct((n, ATT_W), BF16),
        jax.ShapeDtypeStruct((batch, ATT_HEADS * V_ROWS, seq), BF16),
        jax.ShapeDtypeStruct((n, ML_QK_W), BF16),
        jax.ShapeDtypeStruct((batch, ML_QK_W, seq), BF16),
        jax.ShapeDtypeStruct((n, ML_V_W), BF16),
        jax.ShapeDtypeStruct((n, ML_V_W), F32),
        jax.ShapeDtypeStruct((n, LANES), F32),
        jax.ShapeDtypeStruct((batch, GATE_ROWS, seq), F32),
    ]
    out_specs = [tr(ATT_W), row(ATT_W), tr(ATT_HEADS * V_ROWS), row(ML_QK_W), tr(ML_QK_W), row(ML_V_W),
                 row(ML_V_W), row(LANES), tr(GATE_ROWS)]
    return pl.pallas_call(
        _in_proj_kernel,
        grid=(n // tm,),
        in_specs=[row(D_MODEL), full(g), full(w_main), full(w_gate), full(b_gate),
                  row(LANES), row(LANES), row(LANES)],
        out_specs=out_specs,
        out_shape=out_shape,
        compiler_params=pltpu.CompilerParams(dimension_semantics=("arbitrary",),
                                             vmem_limit_bytes=VMEM_LIMIT),
        name="in_proj",
    )(x2d, g, w_main, w_gate, b_gate, cos, sa, sb)


def _moba_kernel(qt_ref, k_ref, vt_ref, o_ref, kmean_ref, qh_ref, sel_ref, acc_ref, s_ref, p_ref,
                 alpha_ref):
    i = pl.program_id(2)
    mb = MOBA_BLOCK
    nb = kmean_ref.shape[0]
    dh = ATT_HEAD_DIM

    @pl.when(i == 0)
    def _():
        for jj in range(nb):
            kb = k_ref[jj * mb:(jj + 1) * mb, :].astype(F32)
            kmean_ref[jj:jj + 1, :] = jnp.sum(kb, axis=0, keepdims=True) * (1.0 / mb)

    qt = qt_ref[0].astype(F32)
    row_head = lax.broadcasted_iota(jnp.int32, qt.shape, 0) // dh
    kmean = kmean_ref[...]
    km_hi = kmean.astype(BF16)
    km_lo = (kmean - km_hi.astype(F32)).astype(BF16)
    blk = lax.broadcasted_iota(jnp.int32, (nb, mb), 0)
    key_pos = lax.broadcasted_iota(jnp.int32, (mb, mb), 0)
    qry_pos = lax.broadcasted_iota(jnp.int32, (mb, mb), 1)
    r0 = pl.multiple_of(i * mb, mb)
    k_own = k_ref[pl.ds(r0, mb), :]

    heads = range(HEADS_PER_GROUP)
    units_per_trip = BLOCKS_PER_TRIP * HEADS_PER_GROUP

    def scores(j, h):
        return jnp.dot(k_ref[pl.ds(pl.multiple_of(j * mb, mb), mb), :], qh_ref[h],
                       preferred_element_type=F32)

    def v_rows(h):
        return slice(h * V_ROWS, (h + 1) * V_ROWS)

    for h in heads:
        qh_ref[h] = jnp.where(row_head == h, qt, 0.0).astype(BF16)
    s_own = [jnp.dot(k_own, qh_ref[h], preferred_element_type=F32) for h in heads]
    gates = [jnp.dot(km_hi, qh_ref[h], preferred_element_type=F32)
             + jnp.dot(km_lo, qh_ref[h], preferred_element_type=F32) for h in heads]
    lead = [scores(0, h) for h in heads]

    m_init = []
    for h in heads:
        g = jnp.where(blk < i, gates[h], NEG_INF)
        sel = blk < 0
        for _ in range(MOBA_TOPK):
            mx = jnp.max(g, axis=0, keepdims=True)
            idx = jnp.min(jnp.where(g == mx, blk, nb), axis=0, keepdims=True)
            pick = blk == idx
            sel = sel | pick
            g = jnp.where(pick, NEG_INF, g)
        sel_ref[h] = jnp.where(sel & (blk < i), 1.0, 0.0)

        s = jnp.where(key_pos <= qry_pos, s_own[h], NEG_INF)
        m0 = jnp.max(s, axis=0, keepdims=True)
        m_init.append(m0)
        p_ref[h] = jnp.exp2(s - m0).astype(BF16)
        alpha_ref[h] = jnp.ones_like(m0)
    acc_ref[...] = jnp.zeros_like(acc_ref)
    for h in heads:
        s_ref[h] = lead[h]

    def add_weighted_values(j, h, alpha, p):
        acc_ref[v_rows(h), :] = alpha * acc_ref[v_rows(h), :] + jnp.dot(
            vt_ref[0, v_rows(h), pl.ds(pl.multiple_of(j * mb, mb), mb)], p, preferred_element_type=F32)

    def flush_deferred(t):
        j = jnp.where(t == 0, i, jnp.minimum(BLOCKS_PER_TRIP * t - 1, nb - 1))
        for h in heads:
            add_weighted_values(j, h, alpha_ref[h], p_ref[h])

    def body(t, ms):
        ms = list(ms)
        flush_deferred(t)
        blocks = [jnp.minimum(BLOCKS_PER_TRIP * t + u, nb - 1) for u in range(BLOCKS_PER_TRIP + 1)]
        units = [(j, h) for j in blocks for h in heads]
        pending = {n: s_ref[n] for n in range(SCORE_LOOKAHEAD)}
        for n in range(units_per_trip):
            if n + SCORE_LOOKAHEAD < units_per_trip:
                pending[n + SCORE_LOOKAHEAD] = scores(*units[n + SCORE_LOOKAHEAD])
            if 0 <= n - NEXT_TRIP_ISSUE_AT < SCORE_LOOKAHEAD:
                nxt = units_per_trip + n - NEXT_TRIP_ISSUE_AT
                pending[nxt] = scores(*units[nxt])
            j, h = units[n]
            s = pending.pop(n)
            picked = sel_ref[h, pl.ds(j, 1), :] > 0.0
            m_new = jnp.where(picked, jnp.maximum(ms[h], jnp.max(s, axis=0, keepdims=True)), ms[h])
            alpha = jnp.exp2(ms[h] - m_new)
            p = jnp.exp2(s - jnp.where(picked, m_new, jnp.inf)).astype(BF16)
            ms[h] = m_new
            if n < units_per_trip - HEADS_PER_GROUP:
                add_weighted_values(j, h, alpha, p)
            else:
                p_ref[h] = p
                alpha_ref[h] = alpha
            done = n - NEXT_TRIP_ISSUE_AT - SCORE_LOOKAHEAD
            if 0 <= done < SCORE_LOOKAHEAD:
                s_ref[done] = pending.pop(units_per_trip + done)
        for n in sorted(k - units_per_trip for k in pending):
            s_ref[n] = pending.pop(units_per_trip + n)
        return tuple(ms)

    trips = (i + BLOCKS_PER_TRIP - 1) // BLOCKS_PER_TRIP
    lax.fori_loop(0, trips, body, tuple(m_init))
    flush_deferred(trips)
    outs = []
    for h in heads:
        a = acc_ref[v_rows(h), :]
        outs.append(a[:dh] * (1.0 / a[dh:dh + 1]))
    o_ref[...] = jnp.concatenate(outs, axis=0).T.astype(BF16)


def _moba(qat, ka, vat, batch, seq):
    n = ka.shape[0]
    mb = MOBA_BLOCK
    nb = seq // mb
    groups = ATT_W // GROUP_W
    return pl.pallas_call(
        _moba_kernel,
        grid=(batch, groups, nb),
        in_specs=[
            pl.BlockSpec((1, GROUP_W, mb), lambda b, g, i: (b, g, i)),
            pl.BlockSpec((seq, GROUP_W), lambda b, g, i: (b, g)),
            pl.BlockSpec((1, HEADS_PER_GROUP * V_ROWS, seq), lambda b, g, i: (b, g, 0)),
        ],
        out_specs=pl.BlockSpec((mb, GROUP_W), lambda b, g, i: (b * nb + i, g)),
        out_shape=jax.ShapeDtypeStruct((n, ATT_W), BF16),
        scratch_shapes=[
            pltpu.VMEM((nb, GROUP_W), F32),
            pltpu.VMEM((HEADS_PER_GROUP, GROUP_W, mb), BF16),
            pltpu.VMEM((HEADS_PER_GROUP, nb, mb), F32),
            pltpu.VMEM((HEADS_PER_GROUP * V_ROWS, mb), F32),
            pltpu.VMEM((SCORE_LOOKAHEAD, mb, mb), F32),
            pltpu.VMEM((HEADS_PER_GROUP, mb, mb), BF16),
            pltpu.VMEM((HEADS_PER_GROUP, 1, mb), F32),
        ],
        compiler_params=pltpu.CompilerParams(
            dimension_semantics=("arbitrary", "arbitrary", "arbitrary"),
            vmem_limit_bytes=VMEM_LIMIT),
        name="moba",
    )(qat, ka, vat)


def _mlstm_kernel(q_ref, kt_ref, v_ref, mo_ref, gc_ref, gt_ref, gout_ref, o_ref, cn_ref, m_ref):
    c = pl.program_id(1)
    L = ROW_TILE
    nh, dk, dv = MLSTM_HEADS, MLSTM_QK_DIM, MLSTM_V_DIM

    @pl.when(c == 0)
    def _():
        cn_ref[...] = jnp.zeros_like(cn_ref)
        m_ref[...] = jnp.zeros_like(m_ref)

    q4 = q_ref[...].astype(F32)
    lane_head = lax.broadcasted_iota(jnp.int32, q4.shape, 1) // dk
    kt4 = kt_ref[0]
    gc = gc_ref[...]
    gt = gt_ref[0]
    t_idx = lax.broadcasted_iota(jnp.int32, (L, L), 0)
    s_idx = lax.broadcasted_iota(jnp.int32, (L, L), 1)
    ones = jnp.ones((L, dv), BF16)
    cn_bf = cn_ref[...].astype(BF16)

    heads = range(nh)
    qhs = [jnp.where(lane_head == h, q4, 0.0).astype(BF16) for h in heads]
    qks = [jnp.dot(qhs[h], kt4, preferred_element_type=F32) for h in heads]
    inters = [jnp.dot(qhs[h], cn_bf, preferred_element_type=F32) for h in heads]
    vexts = [jnp.concatenate([v_ref[:, h * dv:(h + 1) * dv], ones], axis=1) for h in heads]
    m_prevs = [m_ref[h:h + 1, 0:1] for h in heads]
    a_rows = [gt[2 * nh + h:2 * nh + h + 1, :] for h in heads]

    for h in heads:
        m_last = jnp.maximum(m_prevs[h], jnp.max(a_rows[h], axis=1, keepdims=True))
        wg = jnp.exp(a_rows[h] - m_last)
        kw = (kt4[h * dk:(h + 1) * dk, :].astype(F32) * wg).astype(BF16)
        upd = jnp.dot(kw, vexts[h], preferred_element_type=F32)
        rows = slice(h * dk, (h + 1) * dk)
        cn_ref[rows, :] = jnp.exp(m_prevs[h] - m_last) * cn_ref[rows, :] + upd
        b_last = gt[nh + h:nh + h + 1, L - 1:L]
        m_ref[h:h + 1, :] = jnp.broadcast_to(b_last + m_last, (1, LANES))

    lane = lax.broadcasted_iota(jnp.int32, (1, LANES), 1)
    m_lanes = jnp.zeros((1, LANES), F32)
    for h in heads:
        m_lanes = jnp.where(lane == 3 * nh + h, m_prevs[h], m_lanes)
    big_all = jnp.maximum(gc, m_lanes)
    mt_all = pltpu.roll(gc, 2 * nh, 1) + big_all

    big_bs, intras = [], []
    for h in heads:
        col = slice(3 * nh + h, 3 * nh + h + 1)
        big_b = jnp.broadcast_to(big_all[:, col], (L, dv))
        big_b2 = jnp.concatenate([big_b, big_b], axis=1)
        e = jnp.exp(jnp.where(s_idx <= t_idx, a_rows[h] - big_b2, NEG_INF))
        sm = (qks[h] * e).astype(BF16)
        big_bs.append(big_b2)
        intras.append(jnp.dot(sm, vexts[h], preferred_element_type=F32))

    for h in heads:
        col = slice(3 * nh + h, 3 * nh + h + 1)
        tot = intras[h] + jnp.exp(m_prevs[h] - big_bs[h]) * inters[h]
        num, den = tot[:, :dv], tot[:, dv:]
        hh = num / jnp.maximum(jnp.abs(den), jnp.exp(-jnp.broadcast_to(mt_all[:, col], (L, dv))))
        hh = hh * lax.rsqrt(jnp.mean(hh * hh, axis=-1, keepdims=True) + EPS)
        hh = hh * gout_ref[:, h * dv:(h + 1) * dv] * jax.nn.sigmoid(mo_ref[:, h * dv:(h + 1) * dv])
        o_ref[:, h * dv:(h + 1) * dv] = hh.astype(o_ref.dtype)


def _mlstm(qm, kmt, vm, mo, gc, gt, g_out, batch, seq):
    n = qm.shape[0]
    L = ROW_TILE
    nc = seq // L
    row = lambda w: pl.BlockSpec((L, w), lambda b, c: (b * nc + c, 0))
    tr = lambda w: pl.BlockSpec((1, w, L), lambda b, c: (b, 0, c))
    return pl.pallas_call(
        _mlstm_kernel,
        grid=(batch, nc),
        in_specs=[row(ML_QK_W), tr(ML_QK_W), row(ML_V_W), row(ML_V_W), row(LANES), tr(GATE_ROWS),
                  pl.BlockSpec((1, ML_V_W), lambda b, c: (0, 0))],
        out_specs=row(ML_V_W),
        out_shape=jax.ShapeDtypeStruct((n, ML_V_W), BF16),
        scratch_shapes=[
            pltpu.VMEM((ML_QK_W, 2 * MLSTM_V_DIM), F32),
            pltpu.VMEM((8, LANES), F32),
        ],
        compiler_params=pltpu.CompilerParams(dimension_semantics=("arbitrary", "arbitrary"),
                                             vmem_limit_bytes=VMEM_LIMIT),
        name="mlstm",
    )(qm, kmt, vm, mo, gc, gt, g_out)


def _out_mlp_kernel(x_ref, att_ref, hm_ref, wo_ref, gpost_ref, gpre_ref, wup_ref, wdn_ref, gpost2_ref,
                    o_ref):
    mix = (jnp.dot(att_ref[...], wo_ref[:ATT_W, :], preferred_element_type=F32)
           + jnp.dot(hm_ref[...], wo_ref[ATT_W:, :], preferred_element_type=F32))
    x1 = x_ref[...] + _rms(mix, gpost_ref[...])
    hn = _rms(x1, gpre_ref[...]).astype(BF16)
    acc = jnp.zeros(x1.shape, F32)
    for c in range(D_FF // FF_CHUNK):
        cols = slice(c * FF_CHUNK, (c + 1) * FF_CHUNK)
        u = jnp.dot(hn, wup_ref[:, cols], preferred_element_type=F32)
        u = jnp.square(jnp.maximum(u, 0.0)).astype(BF16)
        acc = acc + jnp.dot(u, wdn_ref[cols, :], preferred_element_type=F32)
    o_ref[...] = x1 + _rms(acc, gpost2_ref[...])


def _out_mlp(x2d, att, hm, w_out, g_post, g_pre2, w_up, w_down, g_post2):
    n = x2d.shape[0]
    tm = MLP_ROW_TILE
    row = lambda w: pl.BlockSpec((tm, w), lambda i: (i, 0))
    full = lambda a: pl.BlockSpec(a.shape, lambda i: (0,) * a.ndim)
    return pl.pallas_call(
        _out_mlp_kernel,
        grid=(n // tm,),
        in_specs=[row(D_MODEL), row(ATT_W), row(ML_V_W), full(w_out), full(g_post), full(g_pre2),
                  full(w_up), full(w_down), full(g_post2)],
        out_specs=row(D_MODEL),
        out_shape=jax.ShapeDtypeStruct((n, D_MODEL), F32),
        compiler_params=pltpu.CompilerParams(dimension_semantics=("arbitrary",),
                                             vmem_limit_bytes=VMEM_LIMIT),
        name="out_mlp",
    )(x2d, att, hm, w_out, g_post, g_pre2, w_up, w_down, g_post2)


def kernel(x, positions, g_mix_pre, w_in, b_igate, b_fgate, g_mlstm_out, w_out, g_mix_post, g_mlp_pre,
           w_up, w_down, g_mlp_post):
    batch, seq, d = x.shape
    assert d == D_MODEL and seq % ROW_TILE == 0 and w_in.shape[-1] == MAIN_W + 2 * MLSTM_HEADS
    depth = w_in.shape[0]
    cos, sa, sb = _rope_tables(positions)
    xs = x.reshape(batch * seq, d)
    pad = LANES - 2 * MLSTM_HEADS
    for layer in range(depth):
        w_main = w_in[layer, :, :MAIN_W].astype(BF16)
        w_gate = jnp.pad(w_in[layer, :, MAIN_W:], ((0, 0), (0, pad))).astype(BF16)
        b_gate = jnp.pad(jnp.concatenate([b_igate[layer], b_fgate[layer]]), (0, pad))[None, :].astype(F32)
        qat, ka, vat, qm, kmt, vm, mo, gc, gt = _in_proj(
            xs, g_mix_pre[layer][None, :], w_main, w_gate, b_gate, cos, sa, sb, batch, seq)
        att = _moba(qat, ka, vat, batch, seq)
        hm = _mlstm(qm, kmt, vm, mo, gc, gt, g_mlstm_out[layer][None, :], batch, seq)
        xs = _out_mlp(xs, att, hm, w_out[layer].astype(BF16), g_mix_post[layer][None, :],
                      g_mlp_pre[layer][None, :], w_up[layer].astype(BF16), w_down[layer].astype(BF16),
                      g_mlp_post[layer][None, :])
    return xs.reshape(batch, seq, d)
```

```python
import jax
import jax.numpy as jnp
import numpy as np
from jax import lax
from jax.experimental import pallas as pl
from jax.experimental.pallas import tpu as pltpu

F32 = jnp.float32
BF16 = jnp.bfloat16

D_MODEL = 1024
ATT_HEADS = 8
ATT_HEAD_DIM = 64
ATT_W = ATT_HEADS * ATT_HEAD_DIM
ROPE_DIM = ATT_HEAD_DIM // 4
ROPE_THETA = 500000.0
MOBA_BLOCK = 256
MOBA_TOPK = 3
MLSTM_HEADS = 4
MLSTM_QK_DIM = 64
MLSTM_V_DIM = 128
ML_QK_W = MLSTM_HEADS * MLSTM_QK_DIM
ML_V_W = MLSTM_HEADS * MLSTM_V_DIM
D_FF = 4 * D_MODEL
EPS = 1e-6

LANES = 128
ROW_TILE = MOBA_BLOCK
IN_ROW_TILE = 512
MLP_ROW_TILE = 512
HEADS_PER_GROUP = 4
GROUP_W = HEADS_PER_GROUP * ATT_HEAD_DIM
MAIN_W = 3 * ATT_W + 2 * ML_QK_W + 2 * ML_V_W
GATE_ROWS = 16
FF_CHUNK = 512
VMEM_LIMIT = 56 * 1024 * 1024
NEG_INF = float("-inf")
LOG2E = 1.4426950408889634
TILES_PER_STEP = 2
BLOCKS_PER_TRIP = 2
SCORE_LOOKAHEAD = HEADS_PER_GROUP
V_ROWS = ATT_HEAD_DIM + 16


def _rms(x, g):
    return x * lax.rsqrt(jnp.mean(x * x, axis=-1, keepdims=True) + EPS) * g


def _rope_tables_kernel(pos_ref, invf_ref, ma_ref, mb_ref, cos_ref, sa_ref, sb_ref):
    ang = pos_ref[...].astype(F32) * invf_ref[...]
    sin = jnp.sin(ang)
    cos_ref[...] = jnp.cos(ang)
    sa_ref[...] = sin * ma_ref[...]
    sb_ref[...] = sin * mb_ref[...]


def _rope_tables(positions):
    n = positions.size
    rows = 1024
    inv_freq = ROPE_THETA ** (-jnp.arange(0, ROPE_DIM, 2, dtype=F32) / ROPE_DIM)
    lane = np.arange(LANES) % ATT_HEAD_DIM
    half = ROPE_DIM // 2
    invf = jnp.where(lane < ROPE_DIM, inv_freq[lane % half], 0.0).astype(F32)[None, :]
    ma = jnp.asarray(np.where(lane < half, -1.0, 0.0), F32)[None, :]
    mb = jnp.asarray(np.where((lane >= half) & (lane < ROPE_DIM), 1.0, 0.0), F32)[None, :]
    tab = jax.ShapeDtypeStruct((n, LANES), F32)
    vec = pl.BlockSpec((1, LANES), lambda i: (0, 0))
    out = pl.BlockSpec((rows, LANES), lambda i: (i, 0))
    return pl.pallas_call(
        _rope_tables_kernel,
        grid=(n // rows,),
        in_specs=[pl.BlockSpec((rows, 1), lambda i: (i, 0)), vec, vec, vec],
        out_specs=[out, out, out],
        out_shape=[tab, tab, tab],
        name="rope_tables",
    )(positions.reshape(n, 1), invf, ma, mb)


def _scan_rows(v, op, ident):
    n = v.shape[0]
    rows = lax.broadcasted_iota(jnp.int32, v.shape, 0)
    s = 1
    while s < n:
        shifted = jnp.where(rows >= s, pltpu.roll(v, s, 0), ident)
        v = op(v, shifted)
        s *= 2
    return v


def _rope(t, cos, sa, sb):
    outs = []
    for c in range(t.shape[1] // LANES):
        tc = t[:, c * LANES:(c + 1) * LANES]
        outs.append(tc * cos + pltpu.roll(tc, LANES - ROPE_DIM // 2, 1) * sa
                    + pltpu.roll(tc, ROPE_DIM // 2, 1) * sb)
    return jnp.concatenate(outs, axis=1)


def _in_proj_kernel(x_ref, g_ref, w_ref, wg_ref, bg_ref, cos_ref, sa_ref, sb_ref,
                    qat_ref, ka_ref, vat_ref, qm_ref, kmt_ref, vm_ref, mo_ref, gc_ref, gt_ref):
    h = _rms(x_ref[...], g_ref[...]).astype(BF16)

    def proj(c0, n):
        return jnp.dot(h, w_ref[:, c0:c0 + n], preferred_element_type=F32)

    cos, sa, sb = cos_ref[...], sa_ref[...], sb_ref[...]
    scale = ATT_HEAD_DIM ** -0.5 * LOG2E
    qa = _rope(proj(0, ATT_W), cos, sa, sb) * scale
    qat_ref[0] = qa.T.astype(BF16)
    ka_ref[...] = _rope(proj(ATT_W, ATT_W), cos, sa, sb).astype(BF16)
    vat = proj(2 * ATT_W, ATT_W).T.astype(BF16)
    ones = jnp.ones((V_ROWS - ATT_HEAD_DIM, vat.shape[1]), BF16)
    for hd in range(ATT_HEADS):
        vat_ref[0, hd * V_ROWS:hd * V_ROWS + ATT_HEAD_DIM, :] = vat[hd * ATT_HEAD_DIM:(hd + 1) * ATT_HEAD_DIM]
        vat_ref[0, hd * V_ROWS + ATT_HEAD_DIM:(hd + 1) * V_ROWS, :] = ones
    c0 = 3 * ATT_W
    qm_ref[...] = proj(c0, ML_QK_W).astype(BF16)
    km = proj(c0 + ML_QK_W, ML_QK_W) * (MLSTM_QK_DIM ** -0.5)
    kmt_ref[0] = km.T.astype(BF16)
    vm_ref[...] = proj(c0 + 2 * ML_QK_W, ML_V_W).astype(BF16)
    mo_ref[...] = proj(c0 + 2 * ML_QK_W + ML_V_W, ML_V_W)

    pre_all = jnp.dot(h, wg_ref[...], preferred_element_type=F32) + bg_ref[...]
    nh = MLSTM_HEADS
    L = ROW_TILE
    for c in range(pre_all.shape[0] // L):
        pre = pre_all[c * L:(c + 1) * L]
        b = _scan_rows(jax.nn.log_sigmoid(pre), jnp.add, 0.0)
        a = pltpu.roll(pre, nh, 1) - b
        cm = _scan_rows(a, jnp.maximum, NEG_INF)
        lane = lax.broadcasted_iota(jnp.int32, pre.shape, 1)
        gc = jnp.where(lane < nh, pre,
                       jnp.where(lane < 2 * nh, b,
                                 jnp.where(lane < 3 * nh, pltpu.roll(a, nh, 1), pltpu.roll(cm, 2 * nh, 1))))
        gc_ref[c * L:(c + 1) * L, :] = gc
        gt_ref[0, :, c * L:(c + 1) * L] = gc.T[:GATE_ROWS, :]


def _in_proj(x2d, g, w_main, w_gate, b_gate, cos, sa, sb, batch, seq):
    n = x2d.shape[0]
    tm = IN_ROW_TILE
    nblk = seq // tm
    row = lambda w: pl.BlockSpec((tm, w), lambda i: (i, 0))
    full = lambda a: pl.BlockSpec(a.shape, lambda i: (0,) * a.ndim)
    tr = lambda w: pl.BlockSpec((1, w, tm), lambda i: (i // nblk, 0, i % nblk))
    out_shape = [
        jax.ShapeDtypeStruct((batch, ATT_W, seq), BF16),
        jax.ShapeDtypeStruct((n, ATT_W), BF16),
        jax.ShapeDtypeStruct((batch, ATT_HEADS * V_ROWS, seq), BF16),
        jax.ShapeDtypeStruct((n, ML_QK_W), BF16),
        jax.ShapeDtypeStruct((batch, ML_QK_W, seq), BF16),
        jax.ShapeDtypeStruct((n, ML_V_W), BF16),
        jax.ShapeDtypeStruct((n, ML_V_W), F32),
        jax.ShapeDtypeStruct((n, LANES), F32),
        jax.ShapeDtypeStruct((batch, GATE_ROWS, seq), F32),
    ]
    out_specs = [tr(ATT_W), row(ATT_W), tr(ATT_HEADS * V_ROWS), row(ML_QK_W), tr(ML_QK_W), row(ML_V_W),
                 row(ML_V_W), row(LANES), tr(GATE_ROWS)]
    return pl.pallas_call(
        _in_proj_kernel,
        grid=(n // tm,),
        in_specs=[row(D_MODEL), full(g), full(w_main), full(w_gate), full(b_gate),
                  row(LANES), row(LANES), row(LANES)],
        out_specs=out_specs,
        out_shape=out_shape,
        compiler_params=pltpu.CompilerParams(dimension_semantics=("arbitrary",),
                                             vmem_limit_bytes=VMEM_LIMIT),
        name="in_proj",
    )(x2d, g, w_main, w_gate, b_gate, cos, sa, sb)


def _moba_kernel(qt_ref, k_ref, vt_ref, o_ref, kmean_ref, qh_ref, sel_ref, acc_ref, s_ref, p_ref,
                 alpha_ref):
    step = pl.program_id(2)
    mb = MOBA_BLOCK
    nb = kmean_ref.shape[0]
    dh = ATT_HEAD_DIM
    nh = HEADS_PER_GROUP
    first = TILES_PER_STEP * step
    streams = range(TILES_PER_STEP * nh)
    late = range(nh, 2 * nh)

    @pl.when(step == 0)
    def _():
        for jj in range(nb):
            kb = k_ref[jj * mb:(jj + 1) * mb, :].astype(F32)
            kmean_ref[jj:jj + 1, :] = jnp.sum(kb, axis=0, keepdims=True) * (1.0 / mb)

    def block_cols(j):
        return pl.ds(pl.multiple_of(j * mb, mb), mb)

    def scores(j, c):
        return jnp.dot(k_ref[block_cols(j), :], qh_ref[c], preferred_element_type=F32)

    def acc_rows(c):
        return slice(c * V_ROWS, (c + 1) * V_ROWS)

    def weighted_values(j, c, p):
        h = c % nh
        return jnp.dot(vt_ref[0, h * V_ROWS:(h + 1) * V_ROWS, block_cols(j)], p,
                       preferred_element_type=F32)

    def add_weighted_values(j, c, alpha, p):
        acc_ref[acc_rows(c), :] = alpha * acc_ref[acc_rows(c), :] + weighted_values(j, c, p)

    def masked_softmax(j, c, s, m):
        picked = sel_ref[c, pl.ds(j, 1), :] > 0.0
        m_new = jnp.where(picked, jnp.maximum(m, jnp.max(s, axis=0, keepdims=True)), m)
        alpha = jnp.exp2(m - m_new)
        p = jnp.exp2(s - jnp.where(picked, m_new, jnp.inf)).astype(BF16)
        return m_new, alpha, p

    row_head = lax.broadcasted_iota(jnp.int32, (GROUP_W, mb), 0) // dh
    for c in streams:
        qt = qt_ref[0, :, (c // nh) * mb:(c // nh + 1) * mb].astype(F32)
        qh_ref[c] = jnp.where(row_head == c % nh, qt, 0.0).astype(BF16)

    kmean = kmean_ref[...]
    km_hi = kmean.astype(BF16)
    km_lo = (kmean - km_hi.astype(F32)).astype(BF16)
    gates = [jnp.dot(km_hi, qh_ref[c], preferred_element_type=F32)
             + jnp.dot(km_lo, qh_ref[c], preferred_element_type=F32) for c in streams]

    pro_units = [(first + c // nh, c) for c in streams] + [(first, c) for c in late]
    pro_scores = pro_units + [(0, c) for c in range(SCORE_LOOKAHEAD)]
    pending = {n: scores(*pro_scores[n]) for n in range(SCORE_LOOKAHEAD)}

    blk = lax.broadcasted_iota(jnp.int32, (nb, mb), 0)
    for c in streams:
        past = blk < first + c // nh
        g = jnp.where(past, gates[c], NEG_INF)
        sel = blk < 0
        for _ in range(MOBA_TOPK):
            mx = jnp.max(g, axis=0, keepdims=True)
            idx = jnp.min(jnp.where(g == mx, blk, nb), axis=0, keepdims=True)
            pick = blk == idx
            sel = sel | pick
            g = jnp.where(pick, NEG_INF, g)
        sel_ref[c] = jnp.where(sel & past, 1.0, 0.0)

    key_pos = lax.broadcasted_iota(jnp.int32, (mb, mb), 0)
    qry_pos = lax.broadcasted_iota(jnp.int32, (mb, mb), 1)
    ms = [None] * len(streams)
    for n, (j, c) in enumerate(pro_units):
        if n + SCORE_LOOKAHEAD < len(pro_scores):
            pending[n + SCORE_LOOKAHEAD] = scores(*pro_scores[n + SCORE_LOOKAHEAD])
        s = pending.pop(n)
        if n < len(streams):
            s = jnp.where(key_pos <= qry_pos, s, NEG_INF)
            ms[c] = jnp.max(s, axis=0, keepdims=True)
            acc_ref[acc_rows(c), :] = weighted_values(j, c, jnp.exp2(s - ms[c]).astype(BF16))
        else:
            ms[c], alpha_ref[c - nh], p_ref[c - nh] = masked_softmax(j, c, s, ms[c])
    for n in range(SCORE_LOOKAHEAD):
        s_ref[n] = pending.pop(len(pro_units) + n)

    def flush_deferred(t):
        j = jnp.where(t == 0, first, jnp.minimum(BLOCKS_PER_TRIP * t - 1, nb - 1))
        for c in late:
            add_weighted_values(j, c, alpha_ref[c - nh], p_ref[c - nh])

    units_per_trip = BLOCKS_PER_TRIP * len(streams)

    def body(t, ms):
        ms = list(ms)
        flush_deferred(t)
        blocks = [jnp.minimum(BLOCKS_PER_TRIP * t + u, nb - 1) for u in range(BLOCKS_PER_TRIP + 1)]
        units = [(j, c) for j in blocks for c in streams]
        pending = {n: s_ref[n] for n in range(SCORE_LOOKAHEAD)}
        for n in range(units_per_trip):
            pending[n + SCORE_LOOKAHEAD] = scores(*units[n + SCORE_LOOKAHEAD])
            j, c = units[n]
            ms[c], alpha, p = masked_softmax(j, c, pending.pop(n), ms[c])
            if n < units_per_trip - nh:
                add_weighted_values(j, c, alpha, p)
            else:
                p_ref[c - nh] = p
                alpha_ref[c - nh] = alpha
        for n in range(SCORE_LOOKAHEAD):
            s_ref[n] = pending.pop(units_per_trip + n)
        return tuple(ms)

    trips = (first + BLOCKS_PER_TRIP - 1) // BLOCKS_PER_TRIP
    lax.fori_loop(0, trips, body, tuple(ms))
    flush_deferred(trips)
    for tile in range(TILES_PER_STEP):
        outs = []
        for c in range(tile * nh, (tile + 1) * nh):
            a = acc_ref[acc_rows(c), :]
            outs.append(a[:dh] * (1.0 / a[dh:dh + 1]))
        o_ref[tile * mb:(tile + 1) * mb, :] = jnp.concatenate(outs, axis=0).T.astype(BF16)


def _moba(qat, ka, vat, batch, seq):
    n = ka.shape[0]
    mb = MOBA_BLOCK
    nb = seq // mb
    groups = ATT_W // GROUP_W
    rows = TILES_PER_STEP * mb
    steps = seq // rows
    n_streams = TILES_PER_STEP * HEADS_PER_GROUP
    return pl.pallas_call(
        _moba_kernel,
        grid=(batch, groups, steps),
        in_specs=[
            pl.BlockSpec((1, GROUP_W, rows), lambda b, g, i: (b, g, i)),
            pl.BlockSpec((seq, GROUP_W), lambda b, g, i: (b, g)),
            pl.BlockSpec((1, HEADS_PER_GROUP * V_ROWS, seq), lambda b, g, i: (b, g, 0)),
        ],
        out_specs=pl.BlockSpec((rows, GROUP_W), lambda b, g, i: (b * steps + i, g)),
        out_shape=jax.ShapeDtypeStruct((n, ATT_W), BF16),
        scratch_shapes=[
            pltpu.VMEM((nb, GROUP_W), F32),
            pltpu.VMEM((n_streams, GROUP_W, mb), BF16),
            pltpu.VMEM((n_streams, nb, mb), F32),
            pltpu.VMEM((n_streams * V_ROWS, mb), F32),
            pltpu.VMEM((SCORE_LOOKAHEAD, mb, mb), F32),
            pltpu.VMEM((HEADS_PER_GROUP, mb, mb), BF16),
            pltpu.VMEM((HEADS_PER_GROUP, 1, mb), F32),
        ],
        compiler_params=pltpu.CompilerParams(
            dimension_semantics=("arbitrary", "arbitrary", "arbitrary"),
            vmem_limit_bytes=VMEM_LIMIT),
        name="moba",
    )(qat, ka, vat)


def _mlstm_kernel(q_ref, kt_ref, v_ref, mo_ref, gc_ref, gt_ref, gout_ref, o_ref, cn_ref, m_ref):
    c = pl.program_id(1)
    L = ROW_TILE
    nh, dk, dv = MLSTM_HEADS, MLSTM_QK_DIM, MLSTM_V_DIM

    @pl.when(c == 0)
    def _():
        cn_ref[...] = jnp.zeros_like(cn_ref)
        m_ref[...] = jnp.zeros_like(m_ref)

    q4 = q_ref[...].astype(F32)
    lane_head = lax.broadcasted_iota(jnp.int32, q4.shape, 1) // dk
    kt4 = kt_ref[0]
    gc = gc_ref[...]
    gt = gt_ref[0]
    t_idx = lax.broadcasted_iota(jnp.int32, (L, L), 0)
    s_idx = lax.broadcasted_iota(jnp.int32, (L, L), 1)
    ones = jnp.ones((L, dv), BF16)
    cn_bf = cn_ref[...].astype(BF16)

    heads = range(nh)
    qhs = [jnp.where(lane_head == h, q4, 0.0).astype(BF16) for h in heads]
    qks = [jnp.dot(qhs[h], kt4, preferred_element_type=F32) for h in heads]
    inters = [jnp.dot(qhs[h], cn_bf, preferred_element_type=F32) for h in heads]
    vexts = [jnp.concatenate([v_ref[:, h * dv:(h + 1) * dv], ones], axis=1) for h in heads]
    m_prevs = [m_ref[h:h + 1, 0:1] for h in heads]
    a_rows = [gt[2 * nh + h:2 * nh + h + 1, :] for h in heads]

    for h in heads:
        m_last = jnp.maximum(m_prevs[h], jnp.max(a_rows[h], axis=1, keepdims=True))
        wg = jnp.exp(a_rows[h] - m_last)
        kw = (kt4[h * dk:(h + 1) * dk, :].astype(F32) * wg).astype(BF16)
        upd = jnp.dot(kw, vexts[h], preferred_element_type=F32)
        rows = slice(h * dk, (h + 1) * dk)
        cn_ref[rows, :] = jnp.exp(m_prevs[h] - m_last) * cn_ref[rows, :] + upd
        b_last = gt[nh + h:nh + h + 1, L - 1:L]
        m_ref[h:h + 1, :] = jnp.broadcast_to(b_last + m_last, (1, LANES))

    lane = lax.broadcasted_iota(jnp.int32, (1, LANES), 1)
    m_lanes = jnp.zeros((1, LANES), F32)
    for h in heads:
        m_lanes = jnp.where(lane == 3 * nh + h, m_prevs[h], m_lanes)
    big_all = jnp.maximum(gc, m_lanes)
    mt_all = pltpu.roll(gc, 2 * nh, 1) + big_all

    big_bs, intras = [], []
    for h in heads:
        col = slice(3 * nh + h, 3 * nh + h + 1)
        big_b = jnp.broadcast_to(big_all[:, col], (L, dv))
        big_b2 = jnp.concatenate([big_b, big_b], axis=1)
        e = jnp.exp(jnp.where(s_idx <= t_idx, a_rows[h] - big_b2, NEG_INF))
        sm = (qks[h] * e).astype(BF16)
        big_bs.append(big_b2)
        intras.append(jnp.dot(sm, vexts[h], preferred_element_type=F32))

    for h in heads:
        col = slice(3 * nh + h, 3 * nh + h + 1)
        tot = intras[h] + jnp.exp(m_prevs[h] - big_bs[h]) * inters[h]
        num, den = tot[:, :dv], tot[:, dv:]
        hh = num / jnp.maximum(jnp.abs(den), jnp.exp(-jnp.broadcast_to(mt_all[:, col], (L, dv))))
        hh = hh * lax.rsqrt(jnp.mean(hh * hh, axis=-1, keepdims=True) + EPS)
        hh = hh * gout_ref[:, h * dv:(h + 1) * dv] * jax.nn.sigmoid(mo_ref[:, h * dv:(h + 1) * dv])
        o_ref[:, h * dv:(h + 1) * dv] = hh.astype(o_ref.dtype)


def _mlstm(qm, kmt, vm, mo, gc, gt, g_out, batch, seq):
    n = qm.shape[0]
    L = ROW_TILE
    nc = seq // L
    row = lambda w: pl.BlockSpec((L, w), lambda b, c: (b * nc + c, 0))
    tr = lambda w: pl.BlockSpec((1, w, L), lambda b, c: (b, 0, c))
    return pl.pallas_call(
        _mlstm_kernel,
        grid=(batch, nc),
        in_specs=[row(ML_QK_W), tr(ML_QK_W), row(ML_V_W), row(ML_V_W), row(LANES), tr(GATE_ROWS),
                  pl.BlockSpec((1, ML_V_W), lambda b, c: (0, 0))],
        out_specs=row(ML_V_W),
        out_shape=jax.ShapeDtypeStruct((n, ML_V_W), BF16),
        scratch_shapes=[
            pltpu.VMEM((ML_QK_W, 2 * MLSTM_V_DIM), F32),
            pltpu.VMEM((8, LANES), F32),
        ],
        compiler_params=pltpu.CompilerParams(dimension_semantics=("arbitrary", "arbitrary"),
                                             vmem_limit_bytes=VMEM_LIMIT),
        name="mlstm",
    )(qm, kmt, vm, mo, gc, gt, g_out)


def _out_mlp_kernel(x_ref, att_ref, hm_ref, wo_ref, gpost_ref, gpre_ref, wup_ref, wdn_ref, gpost2_ref,
                    o_ref):
    mix = (jnp.dot(att_ref[...], wo_ref[:ATT_W, :], preferred_element_type=F32)
           + jnp.dot(hm_ref[...], wo_ref[ATT_W:, :], preferred_element_type=F32))
    x1 = x_ref[...] + _rms(mix, gpost_ref[...])
    hn = _rms(x1, gpre_ref[...]).astype(BF16)
    acc = jnp.zeros(x1.shape, F32)
    for c in range(D_FF // FF_CHUNK):
        cols = slice(c * FF_CHUNK, (c + 1) * FF_CHUNK)
        u = jnp.dot(hn, wup_ref[:, cols], preferred_element_type=F32)
        u = jnp.square(jnp.maximum(u, 0.0)).astype(BF16)
        acc = acc + jnp.dot(u, wdn_ref[cols, :], preferred_element_type=F32)
    o_ref[...] = x1 + _rms(acc, gpost2_ref[...])


def _out_mlp(x2d, att, hm, w_out, g_post, g_pre2, w_up, w_down, g_post2):
    n = x2d.shape[0]
    tm = MLP_ROW_TILE
    row = lambda w: pl.BlockSpec((tm, w), lambda i: (i, 0))
    full = lambda a: pl.BlockSpec(a.shape, lambda i: (0,) * a.ndim)
    return pl.pallas_call(
        _out_mlp_kernel,
        grid=(n // tm,),
        in_specs=[row(D_MODEL), row(ATT_W), row(ML_V_W), full(w_out), full(g_post), full(g_pre2),
                  full(w_up), full(w_down), full(g_post2)],
        out_specs=row(D_MODEL),
        out_shape=jax.ShapeDtypeStruct((n, D_MODEL), F32),
        compiler_params=pltpu.CompilerParams(dimension_semantics=("arbitrary",),
                                             vmem_limit_bytes=VMEM_LIMIT),
        name="out_mlp",
    )(x2d, att, hm, w_out, g_post, g_pre2, w_up, w_down, g_post2)


def kernel(x, positions, g_mix_pre, w_in, b_igate, b_fgate, g_mlstm_out, w_out, g_mix_post, g_mlp_pre,
           w_up, w_down, g_mlp_post):
    batch, seq, d = x.shape
    assert d == D_MODEL and w_in.shape[-1] == MAIN_W + 2 * MLSTM_HEADS
    assert seq % max(IN_ROW_TILE, MLP_ROW_TILE, TILES_PER_STEP * MOBA_BLOCK) == 0
    depth = w_in.shape[0]
    cos, sa, sb = _rope_tables(positions)
    xs = x.reshape(batch * seq, d)
    pad = LANES - 2 * MLSTM_HEADS
    for layer in range(depth):
        w_main = w_in[layer, :, :MAIN_W].astype(BF16)
        w_gate = jnp.pad(w_in[layer, :, MAIN_W:], ((0, 0), (0, pad))).astype(BF16)
        b_gate = jnp.pad(jnp.concatenate([b_igate[layer], b_fgate[layer]]), (0, pad))[None, :].astype(F32)
        qat, ka, vat, qm, kmt, vm, mo, gc, gt = _in_proj(
            xs, g_mix_pre[layer][None, :], w_main, w_gate, b_gate, cos, sa, sb, batch, seq)
        att = _moba(qat, ka, vat, batch, seq)
        hm = _mlstm(qm, kmt, vm, mo, gc, gt, g_mlstm_out[layer][None, :], batch, seq)
        xs = _out_mlp(xs, att, hm, w_out[layer].astype(BF16), g_mix_post[layer][None, :],
                      g_mlp_pre[layer][None, :], w_up[layer].astype(BF16), w_down[layer].astype(BF16),
                      g_mlp_post[layer][None, :])
    return xs.reshape(batch, seq, d)
```

```python
import jax
import jax.numpy as jnp
import numpy as np
from jax import lax
from jax.experimental import pallas as pl
from jax.experimental.pallas import tpu as pltpu

F32 = jnp.float32
BF16 = jnp.bfloat16

D_MODEL = 1024
ATT_HEADS = 8
ATT_HEAD_DIM = 64
ATT_W = ATT_HEADS * ATT_HEAD_DIM
ROPE_DIM = ATT_HEAD_DIM // 4
ROPE_THETA = 500000.0
MOBA_BLOCK = 256
MOBA_TOPK = 3
MLSTM_HEADS = 4
MLSTM_QK_DIM = 64
MLSTM_V_DIM = 128
ML_QK_W = MLSTM_HEADS * MLSTM_QK_DIM
ML_V_W = MLSTM_HEADS * MLSTM_V_DIM
D_FF = 4 * D_MODEL
EPS = 1e-6

LANES = 128
ROW_TILE = MOBA_BLOCK
IN_ROW_TILE = 512
MLP_ROW_TILE = 512
HEADS_PER_GROUP = 4
GROUP_W = HEADS_PER_GROUP * ATT_HEAD_DIM
MAIN_W = 3 * ATT_W + 2 * ML_QK_W + 2 * ML_V_W
GATE_ROWS = 16
FF_CHUNK = 512
VMEM_LIMIT = 56 * 1024 * 1024
NEG_INF = float("-inf")
LOG2E = 1.4426950408889634
TILES_PER_STEP = 2
BLOCKS_PER_TRIP = 2
SCORE_LOOKAHEAD = HEADS_PER_GROUP
V_ROWS = ATT_HEAD_DIM + 16
MAX_STABILISER_SLACK = 90.0
BOUND_INFLATE = 1.02


def _rms(x, g):
    return x * lax.rsqrt(jnp.mean(x * x, axis=-1, keepdims=True) + EPS) * g


def _rope_tables_kernel(pos_ref, invf_ref, ma_ref, mb_ref, cos_ref, sa_ref, sb_ref):
    ang = pos_ref[...].astype(F32) * invf_ref[...]
    sin = jnp.sin(ang)
    cos_ref[...] = jnp.cos(ang)
    sa_ref[...] = sin * ma_ref[...]
    sb_ref[...] = sin * mb_ref[...]


def _rope_tables(positions):
    n = positions.size
    rows = 1024
    inv_freq = ROPE_THETA ** (-jnp.arange(0, ROPE_DIM, 2, dtype=F32) / ROPE_DIM)
    lane = np.arange(LANES) % ATT_HEAD_DIM
    half = ROPE_DIM // 2
    invf = jnp.where(lane < ROPE_DIM, inv_freq[lane % half], 0.0).astype(F32)[None, :]
    ma = jnp.asarray(np.where(lane < half, -1.0, 0.0), F32)[None, :]
    mb = jnp.asarray(np.where((lane >= half) & (lane < ROPE_DIM), 1.0, 0.0), F32)[None, :]
    tab = jax.ShapeDtypeStruct((n, LANES), F32)
    vec = pl.BlockSpec((1, LANES), lambda i: (0, 0))
    out = pl.BlockSpec((rows, LANES), lambda i: (i, 0))
    return pl.pallas_call(
        _rope_tables_kernel,
        grid=(n // rows,),
        in_specs=[pl.BlockSpec((rows, 1), lambda i: (i, 0)), vec, vec, vec],
        out_specs=[out, out, out],
        out_shape=[tab, tab, tab],
        name="rope_tables",
    )(positions.reshape(n, 1), invf, ma, mb)


def _scan_rows(v, op, ident):
    n = v.shape[0]
    rows = lax.broadcasted_iota(jnp.int32, v.shape, 0)
    s = 1
    while s < n:
        shifted = jnp.where(rows >= s, pltpu.roll(v, s, 0), ident)
        v = op(v, shifted)
        s *= 2
    return v


def _rope(t, cos, sa, sb):
    outs = []
    for c in range(t.shape[1] // LANES):
        tc = t[:, c * LANES:(c + 1) * LANES]
        outs.append(tc * cos + pltpu.roll(tc, LANES - ROPE_DIM // 2, 1) * sa
                    + pltpu.roll(tc, ROPE_DIM // 2, 1) * sb)
    return jnp.concatenate(outs, axis=1)


def _in_proj_kernel(x_ref, g_ref, w_ref, wg_ref, bg_ref, cos_ref, sa_ref, sb_ref,
                    qat_ref, ka_ref, vat_ref, qm_ref, kmt_ref, vm_ref, mo_ref, gc_ref, gt_ref):
    h = _rms(x_ref[...], g_ref[...]).astype(BF16)

    def proj(c0, n):
        return jnp.dot(h, w_ref[:, c0:c0 + n], preferred_element_type=F32)

    cos, sa, sb = cos_ref[...], sa_ref[...], sb_ref[...]
    scale = ATT_HEAD_DIM ** -0.5 * LOG2E
    qa = _rope(proj(0, ATT_W), cos, sa, sb) * scale
    qat_ref[0] = qa.T.astype(BF16)
    ka_ref[...] = _rope(proj(ATT_W, ATT_W), cos, sa, sb).astype(BF16)
    vat = proj(2 * ATT_W, ATT_W).T.astype(BF16)
    ones = jnp.ones((V_ROWS - ATT_HEAD_DIM, vat.shape[1]), BF16)
    for hd in range(ATT_HEADS):
        vat_ref[0, hd * V_ROWS:hd * V_ROWS + ATT_HEAD_DIM, :] = vat[hd * ATT_HEAD_DIM:(hd + 1) * ATT_HEAD_DIM]
        vat_ref[0, hd * V_ROWS + ATT_HEAD_DIM:(hd + 1) * V_ROWS, :] = ones
    c0 = 3 * ATT_W
    qm_ref[...] = proj(c0, ML_QK_W).astype(BF16)
    km = proj(c0 + ML_QK_W, ML_QK_W) * (MLSTM_QK_DIM ** -0.5)
    kmt_ref[0] = km.T.astype(BF16)
    vm_ref[...] = proj(c0 + 2 * ML_QK_W, ML_V_W).astype(BF16)
    mo_ref[...] = proj(c0 + 2 * ML_QK_W + ML_V_W, ML_V_W)

    pre_all = jnp.dot(h, wg_ref[...], preferred_element_type=F32) + bg_ref[...]
    nh = MLSTM_HEADS
    L = ROW_TILE
    for c in range(pre_all.shape[0] // L):
        pre = pre_all[c * L:(c + 1) * L]
        b = _scan_rows(jax.nn.log_sigmoid(pre), jnp.add, 0.0)
        a = pltpu.roll(pre, nh, 1) - b
        cm = _scan_rows(a, jnp.maximum, NEG_INF)
        lane = lax.broadcasted_iota(jnp.int32, pre.shape, 1)
        gc = jnp.where(lane < nh, pre,
                       jnp.where(lane < 2 * nh, b,
                                 jnp.where(lane < 3 * nh, pltpu.roll(a, nh, 1), pltpu.roll(cm, 2 * nh, 1))))
        gc_ref[c * L:(c + 1) * L, :] = gc
        gt_ref[0, :, c * L:(c + 1) * L] = gc.T[:GATE_ROWS, :]


def _in_proj(x2d, g, w_main, w_gate, b_gate, cos, sa, sb, batch, seq):
    n = x2d.shape[0]
    tm = IN_ROW_TILE
    nblk = seq // tm
    row = lambda w: pl.BlockSpec((tm, w), lambda i: (i, 0))
    full = lambda a: pl.BlockSpec(a.shape, lambda i: (0,) * a.ndim)
    tr = lambda w: pl.BlockSpec((1, w, tm), lambda i: (i // nblk, 0, i % nblk))
    out_shape = [
        jax.ShapeDtypeStruct((batch, ATT_W, seq), BF16),
        jax.ShapeDtypeStruct((n, ATT_W), BF16),
        jax.ShapeDtypeStruct((batch, ATT_HEADS * V_ROWS, seq), BF16),
        jax.ShapeDtypeStruct((n, ML_QK_W), BF16),
        jax.ShapeDtypeStruct((batch, ML_QK_W, seq), BF16),
        jax.ShapeDtypeStruct((n, ML_V_W), BF16),
        jax.ShapeDtypeStruct((n, ML_V_W), F32),
        jax.ShapeDtypeStruct((n, LANES), F32),
        jax.ShapeDtypeStruct((batch, GATE_ROWS, seq), F32),
    ]
    out_specs = [tr(ATT_W), row(ATT_W), tr(ATT_HEADS * V_ROWS), row(ML_QK_W), tr(ML_QK_W), row(ML_V_W),
                 row(ML_V_W), row(LANES), tr(GATE_ROWS)]
    return pl.pallas_call(
        _in_proj_kernel,
        grid=(n // tm,),
        in_specs=[row(D_MODEL), full(g), full(w_main), full(w_gate), full(b_gate),
                  row(LANES), row(LANES), row(LANES)],
        out_specs=out_specs,
        out_shape=out_shape,
        compiler_params=pltpu.CompilerParams(dimension_semantics=("arbitrary",),
                                             vmem_limit_bytes=VMEM_LIMIT),
        name="in_proj",
    )(x2d, g, w_main, w_gate, b_gate, cos, sa, sb)


def _moba_kernel(qt_ref, k_ref, vt_ref, o_ref, kmean_ref, kabs_ref, qh_ref, sel_ref, gate_ref, acc_ref,
                 s_ref, p_ref, alpha_ref):
    step = pl.program_id(2)
    mb = MOBA_BLOCK
    nb = kmean_ref.shape[0]
    dh = ATT_HEAD_DIM
    nh = HEADS_PER_GROUP
    first = TILES_PER_STEP * step
    streams = range(TILES_PER_STEP * nh)
    late = range(nh, 2 * nh)

    @pl.when(step == 0)
    def _():
        for jj in range(nb):
            kb = k_ref[jj * mb:(jj + 1) * mb, :].astype(F32)
            kmean_ref[jj:jj + 1, :] = jnp.sum(kb, axis=0, keepdims=True) * (1.0 / mb)
            kabs_ref[jj:jj + 1, :] = jnp.max(jnp.abs(kb), axis=0, keepdims=True)

    def block_cols(j):
        return pl.ds(pl.multiple_of(j * mb, mb), mb)

    def scores(j, c):
        return jnp.dot(k_ref[block_cols(j), :], qh_ref[c], preferred_element_type=F32)

    def acc_rows(c):
        return slice(c * V_ROWS, (c + 1) * V_ROWS)

    def weighted_values(j, c, p):
        h = c % nh
        return jnp.dot(vt_ref[0, h * V_ROWS:(h + 1) * V_ROWS, block_cols(j)], p,
                       preferred_element_type=F32)

    def add_weighted_values(j, c, alpha, p):
        acc_ref[acc_rows(c), :] = alpha * acc_ref[acc_rows(c), :] + weighted_values(j, c, p)

    def masked_softmax(j, c, s, m, exact_max=True):
        picked = sel_ref[c, pl.ds(j, 1), :] > 0.0
        top = jnp.max(s, axis=0, keepdims=True) if exact_max else gate_ref[c, pl.ds(j, 1), :]
        m_new = jnp.where(picked, jnp.maximum(m, top), m)
        alpha = jnp.exp2(m - m_new)
        p = jnp.exp2(s - jnp.where(picked, m_new, jnp.inf)).astype(BF16)
        return m_new, alpha, p

    row_head = lax.broadcasted_iota(jnp.int32, (GROUP_W, mb), 0) // dh
    for c in streams:
        qt = qt_ref[0, :, (c // nh) * mb:(c // nh + 1) * mb].astype(F32)
        qh_ref[c] = jnp.where(row_head == c % nh, qt, 0.0).astype(BF16)

    kmean = kmean_ref[...]
    km_hi = kmean.astype(BF16)
    km_lo = (kmean - km_hi.astype(F32)).astype(BF16)
    gates = [jnp.dot(km_hi, qh_ref[c], preferred_element_type=F32)
             + jnp.dot(km_lo, qh_ref[c], preferred_element_type=F32) for c in streams]
    kabs = kabs_ref[...].astype(BF16)
    bounds = [jnp.dot(kabs, jnp.abs(qh_ref[c]), preferred_element_type=F32) for c in streams]

    pro_units = [(first + c // nh, c) for c in streams] + [(first, c) for c in late]
    pro_scores = pro_units + [(0, c) for c in range(SCORE_LOOKAHEAD)]
    pending = {n: scores(*pro_scores[n]) for n in range(SCORE_LOOKAHEAD)}

    blk = lax.broadcasted_iota(jnp.int32, (nb, mb), 0)
    for c in streams:
        past = blk < first + c // nh
        g = jnp.where(past, gates[c], NEG_INF)
        sel = blk < 0
        for _ in range(MOBA_TOPK):
            mx = jnp.max(g, axis=0, keepdims=True)
            idx = jnp.min(jnp.where(g == mx, blk, nb), axis=0, keepdims=True)
            pick = blk == idx
            sel = sel | pick
            g = jnp.where(pick, NEG_INF, g)
        sel_ref[c] = jnp.where(sel & past, 1.0, 0.0)
        gate_ref[c] = gates[c]
        slack = jnp.where(sel & past, bounds[c] * BOUND_INFLATE - gates[c], 0.0)
        worst = slack if c == 0 else jnp.maximum(worst, slack)
    mean_is_close = jnp.max(worst) < MAX_STABILISER_SLACK

    key_pos = lax.broadcasted_iota(jnp.int32, (mb, mb), 0)
    qry_pos = lax.broadcasted_iota(jnp.int32, (mb, mb), 1)
    ms = [None] * len(streams)
    for n, (j, c) in enumerate(pro_units):
        if n + SCORE_LOOKAHEAD < len(pro_scores):
            pending[n + SCORE_LOOKAHEAD] = scores(*pro_scores[n + SCORE_LOOKAHEAD])
        s = pending.pop(n)
        if n < len(streams):
            s = jnp.where(key_pos <= qry_pos, s, NEG_INF)
            ms[c] = jnp.max(s, axis=0, keepdims=True)
            acc_ref[acc_rows(c), :] = weighted_values(j, c, jnp.exp2(s - ms[c]).astype(BF16))
        else:
            ms[c], alpha_ref[c - nh], p_ref[c - nh] = masked_softmax(j, c, s, ms[c])
    for n in range(SCORE_LOOKAHEAD):
        s_ref[n] = pending.pop(len(pro_units) + n)

    def flush_deferred(t):
        j = jnp.where(t == 0, first, jnp.minimum(BLOCKS_PER_TRIP * t - 1, nb - 1))
        for c in late:
            add_weighted_values(j, c, alpha_ref[c - nh], p_ref[c - nh])

    units_per_trip = BLOCKS_PER_TRIP * len(streams)

    def body(t, ms, exact_max):
        ms = list(ms)
        flush_deferred(t)
        blocks = [jnp.minimum(BLOCKS_PER_TRIP * t + u, nb - 1) for u in range(BLOCKS_PER_TRIP + 1)]
        units = [(j, c) for j in blocks for c in streams]
        pending = {n: s_ref[n] for n in range(SCORE_LOOKAHEAD)}
        for n in range(units_per_trip):
            pending[n + SCORE_LOOKAHEAD] = scores(*units[n + SCORE_LOOKAHEAD])
            j, c = units[n]
            ms[c], alpha, p = masked_softmax(j, c, pending.pop(n), ms[c], exact_max)
            if n < units_per_trip - nh:
                add_weighted_values(j, c, alpha, p)
            else:
                p_ref[c - nh] = p
                alpha_ref[c - nh] = alpha
        for n in range(SCORE_LOOKAHEAD):
            s_ref[n] = pending.pop(units_per_trip + n)
        return tuple(ms)

    trips = (first + BLOCKS_PER_TRIP - 1) // BLOCKS_PER_TRIP
    lax.cond(mean_is_close,
             lambda: lax.fori_loop(0, trips, lambda t, m: body(t, m, False), tuple(ms)),
             lambda: lax.fori_loop(0, trips, lambda t, m: body(t, m, True), tuple(ms)))
    flush_deferred(trips)
    for tile in range(TILES_PER_STEP):
        outs = []
        for c in range(tile * nh, (tile + 1) * nh):
            a = acc_ref[acc_rows(c), :]
            outs.append(a[:dh] * (1.0 / a[dh:dh + 1]))
        o_ref[tile * mb:(tile + 1) * mb, :] = jnp.concatenate(outs, axis=0).T.astype(BF16)


def _moba(qat, ka, vat, batch, seq):
    n = ka.shape[0]
    mb = MOBA_BLOCK
    nb = seq // mb
    groups = ATT_W // GROUP_W
    rows = TILES_PER_STEP * mb
    steps = seq // rows
    n_streams = TILES_PER_STEP * HEADS_PER_GROUP
    return pl.pallas_call(
        _moba_kernel,
        grid=(batch, groups, steps),
        in_specs=[
            pl.BlockSpec((1, GROUP_W, rows), lambda b, g, i: (b, g, i)),
            pl.BlockSpec((seq, GROUP_W), lambda b, g, i: (b, g)),
            pl.BlockSpec((1, HEADS_PER_GROUP * V_ROWS, seq), lambda b, g, i: (b, g, 0)),
        ],
        out_specs=pl.BlockSpec((rows, GROUP_W), lambda b, g, i: (b * steps + i, g)),
        out_shape=jax.ShapeDtypeStruct((n, ATT_W), BF16),
        scratch_shapes=[
            pltpu.VMEM((nb, GROUP_W), F32),
            pltpu.VMEM((nb, GROUP_W), F32),
            pltpu.VMEM((n_streams, GROUP_W, mb), BF16),
            pltpu.VMEM((n_streams, nb, mb), F32),
            pltpu.VMEM((n_streams, nb, mb), F32),
            pltpu.VMEM((n_streams * V_ROWS, mb), F32),
            pltpu.VMEM((SCORE_LOOKAHEAD, mb, mb), F32),
            pltpu.VMEM((HEADS_PER_GROUP, mb, mb), BF16),
            pltpu.VMEM((HEADS_PER_GROUP, 1, mb), F32),
        ],
        compiler_params=pltpu.CompilerParams(
            dimension_semantics=("arbitrary", "arbitrary", "arbitrary"),
            vmem_limit_bytes=VMEM_LIMIT),
        name="moba",
    )(qat, ka, vat)


def _mlstm_kernel(q_ref, kt_ref, v_ref, mo_ref, gc_ref, gt_ref, gout_ref, o_ref, cn_ref, m_ref):
    c = pl.program_id(1)
    L = ROW_TILE
    nh, dk, dv = MLSTM_HEADS, MLSTM_QK_DIM, MLSTM_V_DIM

    @pl.when(c == 0)
    def _():
        cn_ref[...] = jnp.zeros_like(cn_ref)
        m_ref[...] = jnp.zeros_like(m_ref)

    q4 = q_ref[...].astype(F32)
    lane_head = lax.broadcasted_iota(jnp.int32, q4.shape, 1) // dk
    kt4 = kt_ref[0]
    gc = gc_ref[...]
    gt = gt_ref[0]
    t_idx = lax.broadcasted_iota(jnp.int32, (L, L), 0)
    s_idx = lax.broadcasted_iota(jnp.int32, (L, L), 1)
    ones = jnp.ones((L, dv), BF16)
    cn_bf = cn_ref[...].astype(BF16)

    heads = range(nh)
    qhs = [jnp.where(lane_head == h, q4, 0.0).astype(BF16) for h in heads]
    qks = [jnp.dot(qhs[h], kt4, preferred_element_type=F32) for h in heads]
    inters = [jnp.dot(qhs[h], cn_bf, preferred_element_type=F32) for h in heads]
    vexts = [jnp.concatenate([v_ref[:, h * dv:(h + 1) * dv], ones], axis=1) for h in heads]
    m_prevs = [m_ref[h:h + 1, 0:1] for h in heads]
    a_rows = [gt[2 * nh + h:2 * nh + h + 1, :] for h in heads]

    for h in heads:
        m_last = jnp.maximum(m_prevs[h], jnp.max(a_rows[h], axis=1, keepdims=True))
        wg = jnp.exp(a_rows[h] - m_last)
        kw = (kt4[h * dk:(h + 1) * dk, :].astype(F32) * wg).astype(BF16)
        upd = jnp.dot(kw, vexts[h], preferred_element_type=F32)
        rows = slice(h * dk, (h + 1) * dk)
        cn_ref[rows, :] = jnp.exp(m_prevs[h] - m_last) * cn_ref[rows, :] + upd
        b_last = gt[nh + h:nh + h + 1, L - 1:L]
        m_ref[h:h + 1, :] = jnp.broadcast_to(b_last + m_last, (1, LANES))

    lane = lax.broadcasted_iota(jnp.int32, (1, LANES), 1)
    m_lanes = jnp.zeros((1, LANES), F32)
    for h in heads:
        m_lanes = jnp.where(lane == 3 * nh + h, m_prevs[h], m_lanes)
    big_all = jnp.maximum(gc, m_lanes)
    mt_all = pltpu.roll(gc, 2 * nh, 1) + big_all

    big_bs, intras = [], []
    for h in heads:
        col = slice(3 * nh + h, 3 * nh + h + 1)
        big_b = jnp.broadcast_to(big_all[:, col], (L, dv))
        big_b2 = jnp.concatenate([big_b, big_b], axis=1)
        e = jnp.exp(jnp.where(s_idx <= t_idx, a_rows[h] - big_b2, NEG_INF))
        sm = (qks[h] * e).astype(BF16)
        big_bs.append(big_b2)
        intras.append(jnp.dot(sm, vexts[h], preferred_element_type=F32))

    for h in heads:
        col = slice(3 * nh + h, 3 * nh + h + 1)
        tot = intras[h] + jnp.exp(m_prevs[h] - big_bs[h]) * inters[h]
        num, den = tot[:, :dv], tot[:, dv:]
        hh = num / jnp.maximum(jnp.abs(den), jnp.exp(-jnp.broadcast_to(mt_all[:, col], (L, dv))))
        hh = hh * lax.rsqrt(jnp.mean(hh * hh, axis=-1, keepdims=True) + EPS)
        hh = hh * gout_ref[:, h * dv:(h + 1) * dv] * jax.nn.sigmoid(mo_ref[:, h * dv:(h + 1) * dv])
        o_ref[:, h * dv:(h + 1) * dv] = hh.astype(o_ref.dtype)


def _mlstm(qm, kmt, vm, mo, gc, gt, g_out, batch, seq):
    n = qm.shape[0]
    L = ROW_TILE
    nc = seq // L
    row = lambda w: pl.BlockSpec((L, w), lambda b, c: (b * nc + c, 0))
    tr = lambda w: pl.BlockSpec((1, w, L), lambda b, c: (b, 0, c))
    return pl.pallas_call(
        _mlstm_kernel,
        grid=(batch, nc),
        in_specs=[row(ML_QK_W), tr(ML_QK_W), row(ML_V_W), row(ML_V_W), row(LANES), tr(GATE_ROWS),
                  pl.BlockSpec((1, ML_V_W), lambda b, c: (0, 0))],
        out_specs=row(ML_V_W),
        out_shape=jax.ShapeDtypeStruct((n, ML_V_W), BF16),
        scratch_shapes=[
            pltpu.VMEM((ML_QK_W, 2 * MLSTM_V_DIM), F32),
            pltpu.VMEM((8, LANES), F32),
        ],
        compiler_params=pltpu.CompilerParams(dimension_semantics=("arbitrary", "arbitrary"),
                                             vmem_limit_bytes=VMEM_LIMIT),
        name="mlstm",
    )(qm, kmt, vm, mo, gc, gt, g_out)


def _out_mlp_kernel(x_ref, att_ref, hm_ref, wo_ref, gpost_ref, gpre_ref, wup_ref, wdn_ref, gpost2_ref,
                    o_ref):
    mix = (jnp.dot(att_ref[...], wo_ref[:ATT_W, :], preferred_element_type=F32)
           + jnp.dot(hm_ref[...], wo_ref[ATT_W:, :], preferred_element_type=F32))
    x1 = x_ref[...] + _rms(mix, gpost_ref[...])
    hn = _rms(x1, gpre_ref[...]).astype(BF16)
    acc = jnp.zeros(x1.shape, F32)
    for c in range(D_FF // FF_CHUNK):
        cols = slice(c * FF_CHUNK, (c + 1) * FF_CHUNK)
        u = jnp.dot(hn, wup_ref[:, cols], preferred_element_type=F32)
        u = jnp.square(jnp.maximum(u, 0.0)).astype(BF16)
        acc = acc + jnp.dot(u, wdn_ref[cols, :], preferred_element_type=F32)
    o_ref[...] = x1 + _rms(acc, gpost2_ref[...])


def _out_mlp(x2d, att, hm, w_out, g_post, g_pre2, w_up, w_down, g_post2):
    n = x2d.shape[0]
    tm = MLP_ROW_TILE
    row = lambda w: pl.BlockSpec((tm, w), lambda i: (i, 0))
    full = lambda a: pl.BlockSpec(a.shape, lambda i: (0,) * a.ndim)
    return pl.pallas_call(
        _out_mlp_kernel,
        grid=(n // tm,),
        in_specs=[row(D_MODEL), row(ATT_W), row(ML_V_W), full(w_out), full(g_post), full(g_pre2),
                  full(w_up), full(w_down), full(g_post2)],
        out_specs=row(D_MODEL),
        out_shape=jax.ShapeDtypeStruct((n, D_MODEL), F32),
        compiler_params=pltpu.CompilerParams(dimension_semantics=("arbitrary",),
                                             vmem_limit_bytes=VMEM_LIMIT),
        name="out_mlp",
    )(x2d, att, hm, w_out, g_post, g_pre2, w_up, w_down, g_post2)


def kernel(x, positions, g_mix_pre, w_in, b_igate, b_fgate, g_mlstm_out, w_out, g_mix_post, g_mlp_pre,
           w_up, w_down, g_mlp_post):
    batch, seq, d = x.shape
    assert d == D_MODEL and w_in.shape[-1] == MAIN_W + 2 * MLSTM_HEADS
    assert seq % max(IN_ROW_TILE, MLP_ROW_TILE, TILES_PER_STEP * MOBA_BLOCK) == 0
    depth = w_in.shape[0]
    cos, sa, sb = _rope_tables(positions)
    xs = x.reshape(batch * seq, d)
    pad = LANES - 2 * MLSTM_HEADS
    for layer in range(depth):
        w_main = w_in[layer, :, :MAIN_W].astype(BF16)
        w_gate = jnp.pad(w_in[layer, :, MAIN_W:], ((0, 0), (0, pad))).astype(BF16)
        b_gate = jnp.pad(jnp.concatenate([b_igate[layer], b_fgate[layer]]), (0, pad))[None, :].astype(F32)
        qat, ka, vat, qm, kmt, vm, mo, gc, gt = _in_proj(
            xs, g_mix_pre[layer][None, :], w_main, w_gate, b_gate, cos, sa, sb, batch, seq)
        att = _moba(qat, ka, vat, batch, seq)
        hm = _mlstm(qm, kmt, vm, mo, gc, gt, g_mlstm_out[layer][None, :], batch, seq)
        xs = _out_mlp(xs, att, hm, w_out[layer].astype(BF16), g_mix_post[layer][None, :],
                      g_mlp_pre[layer][None, :], w_up[layer].astype(BF16), w_down[layer].astype(BF16),
                      g_mlp_post[layer][None, :])
    return xs.reshape(batch, seq, d)
```

```python
import jax
import jax.numpy as jnp
import numpy as np
from jax import lax
from jax.experimental import pallas as pl
from jax.experimental.pallas import tpu as pltpu

F32 = jnp.float32
BF16 = jnp.bfloat16

D_MODEL = 1024
ATT_HEADS = 8
ATT_HEAD_DIM = 64
ATT_W = ATT_HEADS * ATT_HEAD_DIM
ROPE_DIM = ATT_HEAD_DIM // 4
ROPE_THETA = 500000.0
MOBA_BLOCK = 256
MOBA_TOPK = 3
MLSTM_HEADS = 4
MLSTM_QK_DIM = 64
MLSTM_V_DIM = 128
ML_QK_W = MLSTM_HEADS * MLSTM_QK_DIM
ML_V_W = MLSTM_HEADS * MLSTM_V_DIM
D_FF = 4 * D_MODEL
EPS = 1e-6

LANES = 128
ROW_TILE = MOBA_BLOCK
IN_ROW_TILE = 512
MLP_ROW_TILE = 512
HEADS_PER_GROUP = 4
GROUP_W = HEADS_PER_GROUP * ATT_HEAD_DIM
MAIN_W = 3 * ATT_W + 2 * ML_QK_W + 2 * ML_V_W
GATE_ROWS = 16
FF_CHUNK = 512
VMEM_LIMIT = 56 * 1024 * 1024
NEG_INF = float("-inf")
LOG2E = 1.4426950408889634
TILES_PER_STEP = 2
LONG_TRIP_BLOCKS = 4
SHORT_TRIP_BLOCKS = TILES_PER_STEP
SCORE_LOOKAHEAD = HEADS_PER_GROUP
MAX_STABILISER_SLACK = 90.0
BOUND_INFLATE = 1.02


def _rms(x, g):
    return x * lax.rsqrt(jnp.mean(x * x, axis=-1, keepdims=True) + EPS) * g


def _rope_tables_kernel(pos_ref, invf_ref, ma_ref, mb_ref, cos_ref, sa_ref, sb_ref):
    ang = pos_ref[...].astype(F32) * invf_ref[...]
    sin = jnp.sin(ang)
    cos_ref[...] = jnp.cos(ang)
    sa_ref[...] = sin * ma_ref[...]
    sb_ref[...] = sin * mb_ref[...]


def _rope_tables(positions):
    n = positions.size
    rows = 1024
    inv_freq = ROPE_THETA ** (-jnp.arange(0, ROPE_DIM, 2, dtype=F32) / ROPE_DIM)
    lane = np.arange(LANES) % ATT_HEAD_DIM
    half = ROPE_DIM // 2
    invf = jnp.where(lane < ROPE_DIM, inv_freq[lane % half], 0.0).astype(F32)[None, :]
    ma = jnp.asarray(np.where(lane < half, -1.0, 0.0), F32)[None, :]
    mb = jnp.asarray(np.where((lane >= half) & (lane < ROPE_DIM), 1.0, 0.0), F32)[None, :]
    tab = jax.ShapeDtypeStruct((n, LANES), F32)
    vec = pl.BlockSpec((1, LANES), lambda i: (0, 0))
    out = pl.BlockSpec((rows, LANES), lambda i: (i, 0))
    return pl.pallas_call(
        _rope_tables_kernel,
        grid=(n // rows,),
        in_specs=[pl.BlockSpec((rows, 1), lambda i: (i, 0)), vec, vec, vec],
        out_specs=[out, out, out],
        out_shape=[tab, tab, tab],
        name="rope_tables",
    )(positions.reshape(n, 1), invf, ma, mb)


def _scan_rows(v, op, ident):
    n = v.shape[0]
    rows = lax.broadcasted_iota(jnp.int32, v.shape, 0)
    s = 1
    while s < n:
        shifted = jnp.where(rows >= s, pltpu.roll(v, s, 0), ident)
        v = op(v, shifted)
        s *= 2
    return v


def _rope(t, cos, sa, sb):
    outs = []
    for c in range(t.shape[1] // LANES):
        tc = t[:, c * LANES:(c + 1) * LANES]
        outs.append(tc * cos + pltpu.roll(tc, LANES - ROPE_DIM // 2, 1) * sa
                    + pltpu.roll(tc, ROPE_DIM // 2, 1) * sb)
    return jnp.concatenate(outs, axis=1)


def _in_proj_kernel(x_ref, g_ref, w_ref, wg_ref, bg_ref, cos_ref, sa_ref, sb_ref,
                    qat_ref, ka_ref, vat_ref, qm_ref, kmt_ref, vm_ref, mo_ref, gc_ref, gt_ref):
    h = _rms(x_ref[...], g_ref[...]).astype(BF16)

    def proj(c0, n):
        return jnp.dot(h, w_ref[:, c0:c0 + n], preferred_element_type=F32)

    cos, sa, sb = cos_ref[...], sa_ref[...], sb_ref[...]
    scale = ATT_HEAD_DIM ** -0.5 * LOG2E
    qa = _rope(proj(0, ATT_W), cos, sa, sb) * scale
    qat_ref[0] = qa.T.astype(BF16)
    ka_ref[...] = _rope(proj(ATT_W, ATT_W), cos, sa, sb).astype(BF16)
    vat_ref[0] = proj(2 * ATT_W, ATT_W).T.astype(BF16)
    c0 = 3 * ATT_W
    qm_ref[...] = proj(c0, ML_QK_W).astype(BF16)
    km = proj(c0 + ML_QK_W, ML_QK_W) * (MLSTM_QK_DIM ** -0.5)
    kmt_ref[0] = km.T.astype(BF16)
    vm_ref[...] = proj(c0 + 2 * ML_QK_W, ML_V_W).astype(BF16)
    mo_ref[...] = proj(c0 + 2 * ML_QK_W + ML_V_W, ML_V_W)

    pre_all = jnp.dot(h, wg_ref[...], preferred_element_type=F32) + bg_ref[...]
    nh = MLSTM_HEADS
    L = ROW_TILE
    for c in range(pre_all.shape[0] // L):
        pre = pre_all[c * L:(c + 1) * L]
        b = _scan_rows(jax.nn.log_sigmoid(pre), jnp.add, 0.0)
        a = pltpu.roll(pre, nh, 1) - b
        cm = _scan_rows(a, jnp.maximum, NEG_INF)
        lane = lax.broadcasted_iota(jnp.int32, pre.shape, 1)
        gc = jnp.where(lane < nh, pre,
                       jnp.where(lane < 2 * nh, b,
                                 jnp.where(lane < 3 * nh, pltpu.roll(a, nh, 1), pltpu.roll(cm, 2 * nh, 1))))
        gc_ref[c * L:(c + 1) * L, :] = gc
        gt_ref[0, :, c * L:(c + 1) * L] = gc.T[:GATE_ROWS, :]


def _in_proj(x2d, g, w_main, w_gate, b_gate, cos, sa, sb, batch, seq):
    n = x2d.shape[0]
    tm = IN_ROW_TILE
    nblk = seq // tm
    row = lambda w: pl.BlockSpec((tm, w), lambda i: (i, 0))
    full = lambda a: pl.BlockSpec(a.shape, lambda i: (0,) * a.ndim)
    tr = lambda w: pl.BlockSpec((1, w, tm), lambda i: (i // nblk, 0, i % nblk))
    out_shape = [
        jax.ShapeDtypeStruct((batch, ATT_W, seq), BF16),
        jax.ShapeDtypeStruct((n, ATT_W), BF16),
        jax.ShapeDtypeStruct((batch, ATT_W, seq), BF16),
        jax.ShapeDtypeStruct((n, ML_QK_W), BF16),
        jax.ShapeDtypeStruct((batch, ML_QK_W, seq), BF16),
        jax.ShapeDtypeStruct((n, ML_V_W), BF16),
        jax.ShapeDtypeStruct((n, ML_V_W), F32),
        jax.ShapeDtypeStruct((n, LANES), F32),
        jax.ShapeDtypeStruct((batch, GATE_ROWS, seq), F32),
    ]
    out_specs = [tr(ATT_W), row(ATT_W), tr(ATT_W), row(ML_QK_W), tr(ML_QK_W), row(ML_V_W),
                 row(ML_V_W), row(LANES), tr(GATE_ROWS)]
    return pl.pallas_call(
        _in_proj_kernel,
        grid=(n // tm,),
        in_specs=[row(D_MODEL), full(g), full(w_main), full(w_gate), full(b_gate),
                  row(LANES), row(LANES), row(LANES)],
        out_specs=out_specs,
        out_shape=out_shape,
        compiler_params=pltpu.CompilerParams(dimension_semantics=("arbitrary",),
                                             vmem_limit_bytes=VMEM_LIMIT),
        name="in_proj",
    )(x2d, g, w_main, w_gate, b_gate, cos, sa, sb)


def _moba_kernel(qt_ref, k_ref, vt_ref, o_ref, kmean_ref, kabs_ref, qh_ref, sel_ref, gate_ref, acc_ref,
                 s_ref, p_ref, alpha_ref):
    step = pl.program_id(2)
    mb = MOBA_BLOCK
    nb = kmean_ref.shape[0]
    dh = ATT_HEAD_DIM
    nh = HEADS_PER_GROUP
    first = TILES_PER_STEP * step
    streams = range(TILES_PER_STEP * nh)
    late = range(nh, 2 * nh)

    @pl.when(step == 0)
    def _():
        for jj in range(nb):
            kb = k_ref[jj * mb:(jj + 1) * mb, :].astype(F32)
            kmean_ref[jj:jj + 1, :] = jnp.sum(kb, axis=0, keepdims=True) * (1.0 / mb)
            kabs_ref[jj:jj + 1, :] = jnp.max(jnp.abs(kb), axis=0, keepdims=True)

    def block_cols(j):
        return pl.ds(pl.multiple_of(j * mb, mb), mb)

    def scores(j, c):
        return jnp.dot(k_ref[block_cols(j), :], qh_ref[c], preferred_element_type=F32)

    def acc_rows(c):
        return slice(c * dh, (c + 1) * dh)

    def weighted_values(j, c, p):
        h = c % nh
        return jnp.dot(vt_ref[0, h * dh:(h + 1) * dh, block_cols(j)], p, preferred_element_type=F32)

    def add_weighted_values(j, c, alpha, p):
        acc_ref[acc_rows(c), :] = alpha * acc_ref[acc_rows(c), :] + weighted_values(j, c, p)

    def masked_softmax(j, c, s, m, l, exact_max=True):
        picked = sel_ref[c, pl.ds(j, 1), :] > 0.0
        top = jnp.max(s, axis=0, keepdims=True) if exact_max else gate_ref[c, pl.ds(j, 1), :]
        m_new = jnp.where(picked, jnp.maximum(m, top), m)
        alpha = jnp.exp2(m - m_new)
        p = jnp.exp2(s - jnp.where(picked, m_new, jnp.inf))
        return m_new, alpha * l + jnp.sum(p, axis=0, keepdims=True), alpha, p.astype(BF16)

    row_head = lax.broadcasted_iota(jnp.int32, (GROUP_W, mb), 0) // dh
    for c in streams:
        qt = qt_ref[0, :, (c // nh) * mb:(c // nh + 1) * mb].astype(F32)
        qh_ref[c] = jnp.where(row_head == c % nh, qt, 0.0).astype(BF16)

    kmean = kmean_ref[...]
    km_hi = kmean.astype(BF16)
    km_lo = (kmean - km_hi.astype(F32)).astype(BF16)
    gates = [jnp.dot(km_hi, qh_ref[c], preferred_element_type=F32)
             + jnp.dot(km_lo, qh_ref[c], preferred_element_type=F32) for c in streams]
    kabs = kabs_ref[...].astype(BF16)
    bounds = [jnp.dot(kabs, jnp.abs(qh_ref[c]), preferred_element_type=F32) for c in streams]

    pro_units = [(first + c // nh, c) for c in streams] + [(first, c) for c in late]
    pro_scores = pro_units + [(0, c) for c in range(SCORE_LOOKAHEAD)]
    pending = {n: scores(*pro_scores[n]) for n in range(SCORE_LOOKAHEAD)}

    blk = lax.broadcasted_iota(jnp.int32, (nb, mb), 0)
    for c in streams:
        past = blk < first + c // nh
        g = jnp.where(past, gates[c], NEG_INF)
        sel = blk < 0
        for _ in range(MOBA_TOPK):
            mx = jnp.max(g, axis=0, keepdims=True)
            idx = jnp.min(jnp.where(g == mx, blk, nb), axis=0, keepdims=True)
            pick = blk == idx
            sel = sel | pick
            g = jnp.where(pick, NEG_INF, g)
        sel_ref[c] = jnp.where(sel & past, 1.0, 0.0)
        gate_ref[c] = gates[c]
        slack = jnp.where(sel & past, bounds[c] * BOUND_INFLATE - gates[c], 0.0)
        worst = slack if c == 0 else jnp.maximum(worst, slack)
    mean_is_close = jnp.max(worst) < MAX_STABILISER_SLACK

    key_pos = lax.broadcasted_iota(jnp.int32, (mb, mb), 0)
    qry_pos = lax.broadcasted_iota(jnp.int32, (mb, mb), 1)
    ms = [None] * len(streams)
    ls = [None] * len(streams)
    for n, (j, c) in enumerate(pro_units):
        if n + SCORE_LOOKAHEAD < len(pro_scores):
            pending[n + SCORE_LOOKAHEAD] = scores(*pro_scores[n + SCORE_LOOKAHEAD])
        s = pending.pop(n)
        if n < len(streams):
            s = jnp.where(key_pos <= qry_pos, s, NEG_INF)
            ms[c] = jnp.max(s, axis=0, keepdims=True)
            p = jnp.exp2(s - ms[c])
            ls[c] = jnp.sum(p, axis=0, keepdims=True)
            acc_ref[acc_rows(c), :] = weighted_values(j, c, p.astype(BF16))
        else:
            ms[c], ls[c], alpha_ref[c - nh], p_ref[c - nh] = masked_softmax(j, c, s, ms[c], ls[c])
    for n in range(SCORE_LOOKAHEAD):
        s_ref[n] = pending.pop(len(pro_units) + n)

    def flush_deferred(start):
        j = jnp.where(start == 0, first, start - 1)
        for c in late:
            add_weighted_values(j, c, alpha_ref[c - nh], p_ref[c - nh])

    def trip(start, n_blocks, state, exact_max):
        ms, ls = list(state[0]), list(state[1])
        flush_deferred(start)
        blocks = [jnp.minimum(start + u, nb - 1) for u in range(n_blocks + 1)]
        units = [(j, c) for j in blocks for c in streams]
        n_units = n_blocks * len(streams)
        pending = {n: s_ref[n] for n in range(SCORE_LOOKAHEAD)}
        for n in range(n_units):
            pending[n + SCORE_LOOKAHEAD] = scores(*units[n + SCORE_LOOKAHEAD])
            j, c = units[n]
            ms[c], ls[c], alpha, p = masked_softmax(j, c, pending.pop(n), ms[c], ls[c], exact_max)
            if n < n_units - nh:
                add_weighted_values(j, c, alpha, p)
            else:
                p_ref[c - nh] = p
                alpha_ref[c - nh] = alpha
        for n in range(SCORE_LOOKAHEAD):
            s_ref[n] = pending.pop(n_units + n)
        return tuple(ms), tuple(ls)

    def past_blocks(exact_max):
        long_trips = first // LONG_TRIP_BLOCKS
        state = lax.fori_loop(0, long_trips,
                              lambda t, st: trip(t * LONG_TRIP_BLOCKS, LONG_TRIP_BLOCKS, st, exact_max),
                              (tuple(ms), tuple(ls)))
        return lax.cond(first % LONG_TRIP_BLOCKS != 0,
                        lambda st: trip(long_trips * LONG_TRIP_BLOCKS, SHORT_TRIP_BLOCKS, st, exact_max),
                        lambda st: st, state)

    _, ls = lax.cond(mean_is_close, lambda: past_blocks(False), lambda: past_blocks(True))
    flush_deferred(first)
    for tile in range(TILES_PER_STEP):
        outs = [acc_ref[acc_rows(c), :] * (1.0 / ls[c]) for c in range(tile * nh, (tile + 1) * nh)]
        o_ref[tile * mb:(tile + 1) * mb, :] = jnp.concatenate(outs, axis=0).T.astype(BF16)


def _moba(qat, ka, vat, batch, seq):
    n = ka.shape[0]
    mb = MOBA_BLOCK
    nb = seq // mb
    groups = ATT_W // GROUP_W
    rows = TILES_PER_STEP * mb
    steps = seq // rows
    n_streams = TILES_PER_STEP * HEADS_PER_GROUP
    return pl.pallas_call(
        _moba_kernel,
        grid=(batch, groups, steps),
        in_specs=[
            pl.BlockSpec((1, GROUP_W, rows), lambda b, g, i: (b, g, i)),
            pl.BlockSpec((seq, GROUP_W), lambda b, g, i: (b, g)),
            pl.BlockSpec((1, GROUP_W, seq), lambda b, g, i: (b, g, 0)),
        ],
        out_specs=pl.BlockSpec((rows, GROUP_W), lambda b, g, i: (b * steps + i, g)),
        out_shape=jax.ShapeDtypeStruct((n, ATT_W), BF16),
        scratch_shapes=[
            pltpu.VMEM((nb, GROUP_W), F32),
            pltpu.VMEM((nb, GROUP_W), F32),
            pltpu.VMEM((n_streams, GROUP_W, mb), BF16),
            pltpu.VMEM((n_streams, nb, mb), F32),
            pltpu.VMEM((n_streams, nb, mb), F32),
            pltpu.VMEM((n_streams * ATT_HEAD_DIM, mb), F32),
            pltpu.VMEM((SCORE_LOOKAHEAD, mb, mb), F32),
            pltpu.VMEM((HEADS_PER_GROUP, mb, mb), BF16),
            pltpu.VMEM((HEADS_PER_GROUP, 1, mb), F32),
        ],
        compiler_params=pltpu.CompilerParams(
            dimension_semantics=("arbitrary", "arbitrary", "arbitrary"),
            vmem_limit_bytes=VMEM_LIMIT),
        name="moba",
    )(qat, ka, vat)


def _mlstm_kernel(q_ref, kt_ref, v_ref, mo_ref, gc_ref, gt_ref, gout_ref, o_ref, cn_ref, m_ref):
    c = pl.program_id(1)
    L = ROW_TILE
    nh, dk, dv = MLSTM_HEADS, MLSTM_QK_DIM, MLSTM_V_DIM

    @pl.when(c == 0)
    def _():
        cn_ref[...] = jnp.zeros_like(cn_ref)
        m_ref[...] = jnp.zeros_like(m_ref)

    q4 = q_ref[...].astype(F32)
    lane_head = lax.broadcasted_iota(jnp.int32, q4.shape, 1) // dk
    kt4 = kt_ref[0]
    gc = gc_ref[...]
    gt = gt_ref[0]
    t_idx = lax.broadcasted_iota(jnp.int32, (L, L), 0)
    s_idx = lax.broadcasted_iota(jnp.int32, (L, L), 1)
    ones = jnp.ones((L, dv), BF16)
    cn_bf = cn_ref[...].astype(BF16)

    heads = range(nh)
    qhs = [jnp.where(lane_head == h, q4, 0.0).astype(BF16) for h in heads]
    qks = [jnp.dot(qhs[h], kt4, preferred_element_type=F32) for h in heads]
    inters = [jnp.dot(qhs[h], cn_bf, preferred_element_type=F32) for h in heads]
    vexts = [jnp.concatenate([v_ref[:, h * dv:(h + 1) * dv], ones], axis=1) for h in heads]
    m_prevs = [m_ref[h:h + 1, 0:1] for h in heads]
    a_rows = [gt[2 * nh + h:2 * nh + h + 1, :] for h in heads]

    for h in heads:
        m_last = jnp.maximum(m_prevs[h], jnp.max(a_rows[h], axis=1, keepdims=True))
        wg = jnp.exp(a_rows[h] - m_last)
        kw = (kt4[h * dk:(h + 1) * dk, :].astype(F32) * wg).astype(BF16)
        upd = jnp.dot(kw, vexts[h], preferred_element_type=F32)
        rows = slice(h * dk, (h + 1) * dk)
        cn_ref[rows, :] = jnp.exp(m_prevs[h] - m_last) * cn_ref[rows, :] + upd
        b_last = gt[nh + h:nh + h + 1, L - 1:L]
        m_ref[h:h + 1, :] = jnp.broadcast_to(b_last + m_last, (1, LANES))

    lane = lax.broadcasted_iota(jnp.int32, (1, LANES), 1)
    m_lanes = jnp.zeros((1, LANES), F32)
    for h in heads:
        m_lanes = jnp.where(lane == 3 * nh + h, m_prevs[h], m_lanes)
    big_all = jnp.maximum(gc, m_lanes)
    mt_all = pltpu.roll(gc, 2 * nh, 1) + big_all

    big_bs, intras = [], []
    for h in heads:
        col = slice(3 * nh + h, 3 * nh + h + 1)
        big_b = jnp.broadcast_to(big_all[:, col], (L, dv))
        big_b2 = jnp.concatenate([big_b, big_b], axis=1)
        e = jnp.exp(jnp.where(s_idx <= t_idx, a_rows[h] - big_b2, NEG_INF))
        sm = (qks[h] * e).astype(BF16)
        big_bs.append(big_b2)
        intras.append(jnp.dot(sm, vexts[h], preferred_element_type=F32))

    for h in heads:
        col = slice(3 * nh + h, 3 * nh + h + 1)
        tot = intras[h] + jnp.exp(m_prevs[h] - big_bs[h]) * inters[h]
        num, den = tot[:, :dv], tot[:, dv:]
        hh = num / jnp.maximum(jnp.abs(den), jnp.exp(-jnp.broadcast_to(mt_all[:, col], (L, dv))))
        hh = hh * lax.rsqrt(jnp.mean(hh * hh, axis=-1, keepdims=True) + EPS)
        hh = hh * gout_ref[:, h * dv:(h + 1) * dv] * jax.nn.sigmoid(mo_ref[:, h * dv:(h + 1) * dv])
        o_ref[:, h * dv:(h + 1) * dv] = hh.astype(o_ref.dtype)


def _mlstm(qm, kmt, vm, mo, gc, gt, g_out, batch, seq):
    n = qm.shape[0]
    L = ROW_TILE
    nc = seq // L
    row = lambda w: pl.BlockSpec((L, w), lambda b, c: (b * nc + c, 0))
    tr = lambda w: pl.BlockSpec((1, w, L), lambda b, c: (b, 0, c))
    return pl.pallas_call(
        _mlstm_kernel,
        grid=(batch, nc),
        in_specs=[row(ML_QK_W), tr(ML_QK_W), row(ML_V_W), row(ML_V_W), row(LANES), tr(GATE_ROWS),
                  pl.BlockSpec((1, ML_V_W), lambda b, c: (0, 0))],
        out_specs=row(ML_V_W),
        out_shape=jax.ShapeDtypeStruct((n, ML_V_W), BF16),
        scratch_shapes=[
            pltpu.VMEM((ML_QK_W, 2 * MLSTM_V_DIM), F32),
            pltpu.VMEM((8, LANES), F32),
        ],
        compiler_params=pltpu.CompilerParams(dimension_semantics=("arbitrary", "arbitrary"),
                                             vmem_limit_bytes=VMEM_LIMIT),
        name="mlstm",
    )(qm, kmt, vm, mo, gc, gt, g_out)


def _out_mlp_kernel(x_ref, att_ref, hm_ref, wo_ref, gpost_ref, gpre_ref, wup_ref, wdn_ref, gpost2_ref,
                    o_ref):
    mix = (jnp.dot(att_ref[...], wo_ref[:ATT_W, :], preferred_element_type=F32)
           + jnp.dot(hm_ref[...], wo_ref[ATT_W:, :], preferred_element_type=F32))
    x1 = x_ref[...] + _rms(mix, gpost_ref[...])
    hn = _rms(x1, gpre_ref[...]).astype(BF16)
    acc = jnp.zeros(x1.shape, F32)
    for c in range(D_FF // FF_CHUNK):
        cols = slice(c * FF_CHUNK, (c + 1) * FF_CHUNK)
        u = jnp.dot(hn, wup_ref[:, cols], preferred_element_type=F32)
        u = jnp.square(jnp.maximum(u, 0.0)).astype(BF16)
        acc = acc + jnp.dot(u, wdn_ref[cols, :], preferred_element_type=F32)
    o_ref[...] = x1 + _rms(acc, gpost2_ref[...])


def _out_mlp(x2d, att, hm, w_out, g_post, g_pre2, w_up, w_down, g_post2):
    n = x2d.shape[0]
    tm = MLP_ROW_TILE
    row = lambda w: pl.BlockSpec((tm, w), lambda i: (i, 0))
    full = lambda a: pl.BlockSpec(a.shape, lambda i: (0,) * a.ndim)
    return pl.pallas_call(
        _out_mlp_kernel,
        grid=(n // tm,),
        in_specs=[row(D_MODEL), row(ATT_W), row(ML_V_W), full(w_out), full(g_post), full(g_pre2),
                  full(w_up), full(w_down), full(g_post2)],
        out_specs=row(D_MODEL),
        out_shape=jax.ShapeDtypeStruct((n, D_MODEL), F32),
        compiler_params=pltpu.CompilerParams(dimension_semantics=("arbitrary",),
                                             vmem_limit_bytes=VMEM_LIMIT),
        name="out_mlp",
    )(x2d, att, hm, w_out, g_post, g_pre2, w_up, w_down, g_post2)


def kernel(x, positions, g_mix_pre, w_in, b_igate, b_fgate, g_mlstm_out, w_out, g_mix_post, g_mlp_pre,
           w_up, w_down, g_mlp_post):
    batch, seq, d = x.shape
    assert d == D_MODEL and w_in.shape[-1] == MAIN_W + 2 * MLSTM_HEADS
    assert seq % max(IN_ROW_TILE, MLP_ROW_TILE, TILES_PER_STEP * MOBA_BLOCK) == 0
    depth = w_in.shape[0]
    cos, sa, sb = _rope_tables(positions)
    xs = x.reshape(batch * seq, d)
    pad = LANES - 2 * MLSTM_HEADS
    for layer in range(depth):
        w_main = w_in[layer, :, :MAIN_W].astype(BF16)
        w_gate = jnp.pad(w_in[layer, :, MAIN_W:], ((0, 0), (0, pad))).astype(BF16)
        b_gate = jnp.pad(jnp.concatenate([b_igate[layer], b_fgate[layer]]), (0, pad))[None, :].astype(F32)
        qat, ka, vat, qm, kmt, vm, mo, gc, gt = _in_proj(
            xs, g_mix_pre[layer][None, :], w_main, w_gate, b_gate, cos, sa, sb, batch, seq)
        att = _moba(qat, ka, vat, batch, seq)
        hm = _mlstm(qm, kmt, vm, mo, gc, gt, g_mlstm_out[layer][None, :], batch, seq)
        xs = _out_mlp(xs, att, hm, w_out[layer].astype(BF16), g_mix_post[layer][None, :],
                      g_mlp_pre[layer][None, :], w_up[layer].astype(BF16), w_down[layer].astype(BF16),
                      g_mlp_post[layer][None, :])
    return xs.reshape(batch, seq, d)
```

```python
import functools

import jax
import jax.numpy as jnp
import numpy as np
from jax import lax
from jax.experimental import pallas as pl
from jax.experimental.pallas import tpu as pltpu

F32 = jnp.float32
BF16 = jnp.bfloat16

D_MODEL = 1024
ATT_HEADS = 8
ATT_HEAD_DIM = 64
ATT_W = ATT_HEADS * ATT_HEAD_DIM
ROPE_DIM = ATT_HEAD_DIM // 4
ROPE_THETA = 500000.0
MOBA_BLOCK = 256
MOBA_TOPK = 3
MLSTM_HEADS = 4
MLSTM_QK_DIM = 64
MLSTM_V_DIM = 128
ML_QK_W = MLSTM_HEADS * MLSTM_QK_DIM
ML_V_W = MLSTM_HEADS * MLSTM_V_DIM
D_FF = 4 * D_MODEL
EPS = 1e-6

LANES = 128
ROW_TILE = MOBA_BLOCK
MLSTM_CHUNKS_PER_STEP = 1
IN_ROW_TILE = 512
MLP_ROW_TILE = 512
HEADS_PER_GROUP = 4
GROUP_W = HEADS_PER_GROUP * ATT_HEAD_DIM
MAIN_W = 3 * ATT_W + 2 * ML_QK_W + 2 * ML_V_W
GATE_ROWS = 16
FF_CHUNK = 512
VMEM_LIMIT = 56 * 1024 * 1024
NEG_INF = float("-inf")
LOG2E = 1.4426950408889634
TILES_PER_STEP = 2
LONG_TRIP_BLOCKS = 4
SHORT_TRIP_BLOCKS = TILES_PER_STEP
SCORE_LOOKAHEAD = HEADS_PER_GROUP
MAX_STABILISER_SLACK = 90.0
BOUND_INFLATE = 1.02


def _rms(x, g):
    return x * lax.rsqrt(jnp.mean(x * x, axis=-1, keepdims=True) + EPS) * g


def _rope_tables_kernel(pos_ref, invf_ref, ma_ref, mb_ref, cos_ref, sa_ref, sb_ref):
    ang = pos_ref[...].astype(F32) * invf_ref[...]
    sin = jnp.sin(ang)
    cos_ref[...] = jnp.cos(ang)
    sa_ref[...] = sin * ma_ref[...]
    sb_ref[...] = sin * mb_ref[...]


def _rope_tables(positions):
    n = positions.size
    rows = 1024
    inv_freq = ROPE_THETA ** (-jnp.arange(0, ROPE_DIM, 2, dtype=F32) / ROPE_DIM)
    lane = np.arange(LANES) % ATT_HEAD_DIM
    half = ROPE_DIM // 2
    invf = jnp.where(lane < ROPE_DIM, inv_freq[lane % half], 0.0).astype(F32)[None, :]
    ma = jnp.asarray(np.where(lane < half, -1.0, 0.0), F32)[None, :]
    mb = jnp.asarray(np.where((lane >= half) & (lane < ROPE_DIM), 1.0, 0.0), F32)[None, :]
    tab = jax.ShapeDtypeStruct((n, LANES), F32)
    vec = pl.BlockSpec((1, LANES), lambda i: (0, 0))
    out = pl.BlockSpec((rows, LANES), lambda i: (i, 0))
    return pl.pallas_call(
        _rope_tables_kernel,
        grid=(n // rows,),
        in_specs=[pl.BlockSpec((rows, 1), lambda i: (i, 0)), vec, vec, vec],
        out_specs=[out, out, out],
        out_shape=[tab, tab, tab],
        name="rope_tables",
    )(positions.reshape(n, 1), invf, ma, mb)


def _scan_rows(v, op, ident):
    n = v.shape[0]
    rows = lax.broadcasted_iota(jnp.int32, v.shape, 0)
    s = 1
    while s < n:
        shifted = jnp.where(rows >= s, pltpu.roll(v, s, 0), ident)
        v = op(v, shifted)
        s *= 2
    return v


def _rope(t, cos, sa, sb):
    outs = []
    for c in range(t.shape[1] // LANES):
        tc = t[:, c * LANES:(c + 1) * LANES]
        outs.append(tc * cos + pltpu.roll(tc, LANES - ROPE_DIM // 2, 1) * sa
                    + pltpu.roll(tc, ROPE_DIM // 2, 1) * sb)
    return jnp.concatenate(outs, axis=1)


def _in_proj_kernel(x_ref, g_ref, w_ref, wg_ref, bg_ref, cos_ref, sa_ref, sb_ref,
                    qat_ref, ka_ref, vat_ref, qm_ref, kmt_ref, vm_ref, mo_ref, gc_ref, gt_ref):
    h = _rms(x_ref[...], g_ref[...]).astype(BF16)

    def proj(c0, n):
        return jnp.dot(h, w_ref[0, :, c0:c0 + n].astype(BF16), preferred_element_type=F32)

    cos, sa, sb = cos_ref[...], sa_ref[...], sb_ref[...]
    scale = ATT_HEAD_DIM ** -0.5 * LOG2E
    qa = _rope(proj(0, ATT_W), cos, sa, sb) * scale
    qat_ref[0] = qa.T.astype(BF16)
    ka_ref[...] = _rope(proj(ATT_W, ATT_W), cos, sa, sb).astype(BF16)
    vat_ref[0] = proj(2 * ATT_W, ATT_W).T.astype(BF16)
    c0 = 3 * ATT_W
    qm_ref[...] = proj(c0, ML_QK_W).astype(BF16)
    km = proj(c0 + ML_QK_W, ML_QK_W) * (MLSTM_QK_DIM ** -0.5)
    kmt_ref[0] = km.T.astype(BF16)
    vm_ref[...] = proj(c0 + 2 * ML_QK_W, ML_V_W).astype(BF16)
    mo_ref[...] = proj(c0 + 2 * ML_QK_W + ML_V_W, ML_V_W)

    pre_all = jnp.dot(h, wg_ref[...], preferred_element_type=F32) + bg_ref[...]
    nh = MLSTM_HEADS
    L = ROW_TILE
    for c in range(pre_all.shape[0] // L):
        pre = pre_all[c * L:(c + 1) * L]
        b = _scan_rows(jax.nn.log_sigmoid(pre), jnp.add, 0.0)
        a = pltpu.roll(pre, nh, 1) - b
        cm = _scan_rows(a, jnp.maximum, NEG_INF)
        lane = lax.broadcasted_iota(jnp.int32, pre.shape, 1)
        gc = jnp.where(lane < nh, pre,
                       jnp.where(lane < 2 * nh, b,
                                 jnp.where(lane < 3 * nh, pltpu.roll(a, nh, 1), pltpu.roll(cm, 2 * nh, 1))))
        gc_ref[c * L:(c + 1) * L, :] = gc
        gt_ref[0, :, c * L:(c + 1) * L] = gc.T[:GATE_ROWS, :]


def _in_proj(x2d, g, w_in, layer, w_gate, b_gate, cos, sa, sb, batch, seq):
    n = x2d.shape[0]
    tm = IN_ROW_TILE
    nblk = seq // tm
    row = lambda w: pl.BlockSpec((tm, w), lambda i: (i, 0))
    full = lambda a: pl.BlockSpec(a.shape, lambda i: (0,) * a.ndim)
    tr = lambda w: pl.BlockSpec((1, w, tm), lambda i: (i // nblk, 0, i % nblk))
    out_shape = [
        jax.ShapeDtypeStruct((batch, ATT_W, seq), BF16),
        jax.ShapeDtypeStruct((n, ATT_W), BF16),
        jax.ShapeDtypeStruct((batch, ATT_W, seq), BF16),
        jax.ShapeDtypeStruct((n, ML_QK_W), BF16),
        jax.ShapeDtypeStruct((batch, ML_QK_W, seq), BF16),
        jax.ShapeDtypeStruct((n, ML_V_W), BF16),
        jax.ShapeDtypeStruct((n, ML_V_W), F32),
        jax.ShapeDtypeStruct((n, LANES), F32),
        jax.ShapeDtypeStruct((batch, GATE_ROWS, seq), F32),
    ]
    out_specs = [tr(ATT_W), row(ATT_W), tr(ATT_W), row(ML_QK_W), tr(ML_QK_W), row(ML_V_W),
                 row(ML_V_W), row(LANES), tr(GATE_ROWS)]
    return pl.pallas_call(
        _in_proj_kernel,
        grid=(n // tm,),
        in_specs=[row(D_MODEL), full(g), pl.BlockSpec((1,) + w_in.shape[1:], lambda i: (layer, 0, 0)),
                  full(w_gate), full(b_gate), row(LANES), row(LANES), row(LANES)],
        out_specs=out_specs,
        out_shape=out_shape,
        compiler_params=pltpu.CompilerParams(dimension_semantics=("arbitrary",),
                                             vmem_limit_bytes=VMEM_LIMIT),
        name="in_proj",
    )(x2d, g, w_in, w_gate, b_gate, cos, sa, sb)


def _moba_kernel(qt_ref, k_ref, vt_ref, o_ref, kmean_ref, knorm_ref, qh_ref, sel_ref, gate_ref, acc_ref,
                 s_ref, p_ref, alpha_ref):
    step = pl.program_id(2)
    mb = MOBA_BLOCK
    nb = kmean_ref.shape[0]
    dh = ATT_HEAD_DIM
    nh = HEADS_PER_GROUP
    first = TILES_PER_STEP * step
    streams = range(TILES_PER_STEP * nh)
    late = range(nh, 2 * nh)

    @pl.when(step == 0)
    def _():
        qh_ref[...] = jnp.zeros_like(qh_ref)
        head_cols = (lax.broadcasted_iota(jnp.int32, (GROUP_W, LANES), 0) // dh
                     == lax.broadcasted_iota(jnp.int32, (GROUP_W, LANES), 1)).astype(BF16)
        for jj in range(nb):
            kb = k_ref[jj * mb:(jj + 1) * mb, :].astype(F32)
            kmean_ref[jj:jj + 1, :] = jnp.sum(kb, axis=0, keepdims=True) * (1.0 / mb)
            norms = jnp.dot((kb * kb).astype(BF16), head_cols, preferred_element_type=F32)
            knorm_ref[jj:jj + 1, :] = jnp.max(norms, axis=0, keepdims=True)

    def block_cols(j):
        return pl.ds(pl.multiple_of(j * mb, mb), mb)

    def scores(j, c):
        return jnp.dot(k_ref[block_cols(j), :], qh_ref[c], preferred_element_type=F32)

    def acc_rows(c):
        return slice(c * dh, (c + 1) * dh)

    def weighted_values(j, c, p):
        h = c % nh
        return jnp.dot(vt_ref[0, h * dh:(h + 1) * dh, block_cols(j)], p, preferred_element_type=F32)

    def add_weighted_values(j, c, alpha, p):
        acc_ref[acc_rows(c), :] = alpha * acc_ref[acc_rows(c), :] + weighted_values(j, c, p)

    def masked_softmax(j, c, s, m, l, exact_max=True):
        picked = sel_ref[c, pl.ds(j, 1), :] > 0.0
        top = jnp.max(s, axis=0, keepdims=True) if exact_max else gate_ref[c, pl.ds(j, 1), :]
        m_new = jnp.where(picked, jnp.maximum(m, top), m)
        alpha = jnp.exp2(m - m_new)
        p = jnp.exp2(s - jnp.where(picked, m_new, jnp.inf))
        return m_new, alpha * l + jnp.sum(p, axis=0, keepdims=True), alpha, p.astype(BF16)

    for c in streams:
        band = slice((c % nh) * dh, (c % nh + 1) * dh)
        qh_ref[c, band, :] = qt_ref[0, band, (c // nh) * mb:(c // nh + 1) * mb]

    kmean = kmean_ref[...]
    km_hi = kmean.astype(BF16)
    km_lo = (kmean - km_hi.astype(F32)).astype(BF16)

    def own_scores_and_gates(j, c):
        lhs = jnp.concatenate([k_ref[block_cols(j), :], km_hi, km_lo], axis=0)
        return jnp.dot(lhs, qh_ref[c], preferred_element_type=F32)

    pro_units = [(first + c // nh, c) for c in streams] + [(first, c) for c in late]
    pro_scores = ([functools.partial(own_scores_and_gates, j, c) for j, c in pro_units[:len(streams)]]
                  + [functools.partial(scores, j, c) for j, c in pro_units[len(streams):]]
                  + [functools.partial(scores, 0, c) for c in range(SCORE_LOOKAHEAD)])
    pending = {n: pro_scores[n]() for n in range(SCORE_LOOKAHEAD)}

    blk = lax.broadcasted_iota(jnp.int32, (nb, mb), 0)

    def select_blocks(c, gate):
        past = blk < first + c // nh
        g = jnp.where(past, gate, NEG_INF)
        sel = blk < 0
        for _ in range(MOBA_TOPK):
            mx = jnp.max(g, axis=0, keepdims=True)
            idx = jnp.min(jnp.where(g == mx, blk, nb), axis=0, keepdims=True)
            pick = blk == idx
            sel = sel | pick
            g = jnp.where(pick, NEG_INF, g)
        sel_ref[c] = jnp.where(sel & past, 1.0, 0.0)
        gate_ref[c] = gate
        h = c % nh
        q = qh_ref[c, h * dh:(h + 1) * dh, :].astype(F32)
        bound = jnp.sqrt(knorm_ref[:, h:h + 1] * jnp.sum(q * q, axis=0, keepdims=True))
        return jnp.where(sel & past, bound * BOUND_INFLATE - gate, 0.0)

    key_pos = lax.broadcasted_iota(jnp.int32, (mb, mb), 0)
    qry_pos = lax.broadcasted_iota(jnp.int32, (mb, mb), 1)
    ms = [None] * len(streams)
    ls = [None] * len(streams)
    worst = None
    for n, (j, c) in enumerate(pro_units):
        if n + SCORE_LOOKAHEAD < len(pro_scores):
            pending[n + SCORE_LOOKAHEAD] = pro_scores[n + SCORE_LOOKAHEAD]()
        s = pending.pop(n)
        if n < len(streams):
            gate = s[mb:mb + nb] + s[mb + nb:]
            s = jnp.where(key_pos <= qry_pos, s[:mb], NEG_INF)
            ms[c] = jnp.max(s, axis=0, keepdims=True)
            p = jnp.exp2(s - ms[c])
            ls[c] = jnp.sum(p, axis=0, keepdims=True)
            acc_ref[acc_rows(c), :] = weighted_values(j, c, p.astype(BF16))
            slack = select_blocks(c, gate)
            worst = slack if worst is None else jnp.maximum(worst, slack)
        else:
            ms[c], ls[c], alpha_ref[c - nh], p_ref[c - nh] = masked_softmax(j, c, s, ms[c], ls[c])
    mean_is_close = jnp.max(worst) < MAX_STABILISER_SLACK
    for n in range(SCORE_LOOKAHEAD):
        s_ref[n] = pending.pop(len(pro_units) + n)

    def flush_deferred(start):
        j = jnp.where(start == 0, first, start - 1)
        for c in late:
            add_weighted_values(j, c, alpha_ref[c - nh], p_ref[c - nh])

    def trip(start, n_blocks, state, exact_max):
        ms, ls = list(state[0]), list(state[1])
        flush_deferred(start)
        blocks = [jnp.minimum(start + u, nb - 1) for u in range(n_blocks + 1)]
        units = [(j, c) for j in blocks for c in streams]
        n_units = n_blocks * len(streams)
        pending = {n: s_ref[n] for n in range(SCORE_LOOKAHEAD)}
        for n in range(n_units):
            pending[n + SCORE_LOOKAHEAD] = scores(*units[n + SCORE_LOOKAHEAD])
            j, c = units[n]
            ms[c], ls[c], alpha, p = masked_softmax(j, c, pending.pop(n), ms[c], ls[c], exact_max)
            if n < n_units - nh:
                add_weighted_values(j, c, alpha, p)
            else:
                p_ref[c - nh] = p
                alpha_ref[c - nh] = alpha
        for n in range(SCORE_LOOKAHEAD):
            s_ref[n] = pending.pop(n_units + n)
        return tuple(ms), tuple(ls)

    def past_blocks(exact_max):
        long_trips = first // LONG_TRIP_BLOCKS
        state = lax.fori_loop(0, long_trips,
                              lambda t, st: trip(t * LONG_TRIP_BLOCKS, LONG_TRIP_BLOCKS, st, exact_max),
                              (tuple(ms), tuple(ls)))
        return lax.cond(first % LONG_TRIP_BLOCKS != 0,
                        lambda st: trip(long_trips * LONG_TRIP_BLOCKS, SHORT_TRIP_BLOCKS, st, exact_max),
                        lambda st: st, state)

    _, ls = lax.cond(mean_is_close, lambda: past_blocks(False), lambda: past_blocks(True))
    flush_deferred(first)
    for tile in range(TILES_PER_STEP):
        outs = [acc_ref[acc_rows(c), :] * (1.0 / ls[c]) for c in range(tile * nh, (tile + 1) * nh)]
        o_ref[tile * mb:(tile + 1) * mb, :] = jnp.concatenate(outs, axis=0).T.astype(BF16)


def _moba(qat, ka, vat, batch, seq):
    n = ka.shape[0]
    mb = MOBA_BLOCK
    nb = seq // mb
    groups = ATT_W // GROUP_W
    rows = TILES_PER_STEP * mb
    steps = seq // rows
    n_streams = TILES_PER_STEP * HEADS_PER_GROUP
    return pl.pallas_call(
        _moba_kernel,
        grid=(batch, groups, steps),
        in_specs=[
            pl.BlockSpec((1, GROUP_W, rows), lambda b, g, i: (b, g, i)),
            pl.BlockSpec((seq, GROUP_W), lambda b, g, i: (b, g)),
            pl.BlockSpec((1, GROUP_W, seq), lambda b, g, i: (b, g, 0)),
        ],
        out_specs=pl.BlockSpec((rows, GROUP_W), lambda b, g, i: (b * steps + i, g)),
        out_shape=jax.ShapeDtypeStruct((n, ATT_W), BF16),
        scratch_shapes=[
            pltpu.VMEM((nb, GROUP_W), F32),
            pltpu.VMEM((nb, LANES), F32),
            pltpu.VMEM((n_streams, GROUP_W, mb), BF16),
            pltpu.VMEM((n_streams, nb, mb), F32),
            pltpu.VMEM((n_streams, nb, mb), F32),
            pltpu.VMEM((n_streams * ATT_HEAD_DIM, mb), F32),
            pltpu.VMEM((SCORE_LOOKAHEAD, mb, mb), F32),
            pltpu.VMEM((HEADS_PER_GROUP, mb, mb), BF16),
            pltpu.VMEM((HEADS_PER_GROUP, 1, mb), F32),
        ],
        compiler_params=pltpu.CompilerParams(
            dimension_semantics=("arbitrary", "arbitrary", "arbitrary"),
            vmem_limit_bytes=VMEM_LIMIT),
        name="moba",
    )(qat, ka, vat)


def _mlstm_kernel(q_ref, kt_ref, v_ref, mo_ref, gc_ref, gt_ref, gout_ref, o_ref, cn_ref, m_ref):
    @pl.when(pl.program_id(1) == 0)
    def _():
        cn_ref[...] = jnp.zeros_like(cn_ref)
        m_ref[...] = jnp.zeros_like(m_ref)

    for k in range(MLSTM_CHUNKS_PER_STEP):
        _mlstm_chunk(slice(k * ROW_TILE, (k + 1) * ROW_TILE), q_ref, kt_ref, v_ref, mo_ref, gc_ref, gt_ref,
                     gout_ref, o_ref, cn_ref, m_ref)


def _mlstm_chunk(rows, q_ref, kt_ref, v_ref, mo_ref, gc_ref, gt_ref, gout_ref, o_ref, cn_ref, m_ref):
    L = ROW_TILE
    nh, dk, dv = MLSTM_HEADS, MLSTM_QK_DIM, MLSTM_V_DIM
    q4 = q_ref[rows, :].astype(F32)
    lane_head = lax.broadcasted_iota(jnp.int32, q4.shape, 1) // dk
    kt4 = kt_ref[0, :, rows]
    gc = gc_ref[rows, :]
    gt = gt_ref[0, :, rows]
    t_idx = lax.broadcasted_iota(jnp.int32, (L, L), 0)
    s_idx = lax.broadcasted_iota(jnp.int32, (L, L), 1)
    ones = jnp.ones((L, dv), BF16)
    cn_bf = cn_ref[...].astype(BF16)

    heads = range(nh)
    qhs = [jnp.where(lane_head == h, q4, 0.0).astype(BF16) for h in heads]
    qks = [jnp.dot(qhs[h], kt4, preferred_element_type=F32) for h in heads]
    inters = [jnp.dot(qhs[h], cn_bf, preferred_element_type=F32) for h in heads]
    vexts = [jnp.concatenate([v_ref[rows, h * dv:(h + 1) * dv], ones], axis=1) for h in heads]
    m_prevs = [m_ref[h:h + 1, 0:1] for h in heads]
    a_rows = [gt[2 * nh + h:2 * nh + h + 1, :] for h in heads]

    for h in heads:
        m_last = jnp.maximum(m_prevs[h], jnp.max(a_rows[h], axis=1, keepdims=True))
        wg = jnp.exp(a_rows[h] - m_last)
        kw = (kt4[h * dk:(h + 1) * dk, :].astype(F32) * wg).astype(BF16)
        upd = jnp.dot(kw, vexts[h], preferred_element_type=F32)
        hrows = slice(h * dk, (h + 1) * dk)
        cn_ref[hrows, :] = jnp.exp(m_prevs[h] - m_last) * cn_ref[hrows, :] + upd
        b_last = gt[nh + h:nh + h + 1, L - 1:L]
        m_ref[h:h + 1, :] = jnp.broadcast_to(b_last + m_last, (1, LANES))

    lane = lax.broadcasted_iota(jnp.int32, (1, LANES), 1)
    m_lanes = jnp.zeros((1, LANES), F32)
    for h in heads:
        m_lanes = jnp.where(lane == 3 * nh + h, m_prevs[h], m_lanes)
    big_all = jnp.maximum(gc, m_lanes)
    mt_all = pltpu.roll(gc, 2 * nh, 1) + big_all

    big_bs, intras = [], []
    for h in heads:
        col = slice(3 * nh + h, 3 * nh + h + 1)
        big_b = jnp.broadcast_to(big_all[:, col], (L, dv))
        big_b2 = jnp.concatenate([big_b, big_b], axis=1)
        e = jnp.exp(jnp.where(s_idx <= t_idx, a_rows[h] - big_b2, NEG_INF))
        sm = (qks[h] * e).astype(BF16)
        big_bs.append(big_b2)
        intras.append(jnp.dot(sm, vexts[h], preferred_element_type=F32))

    for h in heads:
        col = slice(3 * nh + h, 3 * nh + h + 1)
        hcols = slice(h * dv, (h + 1) * dv)
        tot = intras[h] + jnp.exp(m_prevs[h] - big_bs[h]) * inters[h]
        num, den = tot[:, :dv], tot[:, dv:]
        hh = num / jnp.maximum(jnp.abs(den), jnp.exp(-jnp.broadcast_to(mt_all[:, col], (L, dv))))
        hh = hh * lax.rsqrt(jnp.mean(hh * hh, axis=-1, keepdims=True) + EPS)
        hh = hh * gout_ref[:, hcols] * jax.nn.sigmoid(mo_ref[rows, hcols])
        o_ref[rows, hcols] = hh.astype(o_ref.dtype)


def _mlstm(qm, kmt, vm, mo, gc, gt, g_out, batch, seq):
    n = qm.shape[0]
    L = MLSTM_CHUNKS_PER_STEP * ROW_TILE
    nc = seq // L
    row = lambda w: pl.BlockSpec((L, w), lambda b, c: (b * nc + c, 0))
    tr = lambda w: pl.BlockSpec((1, w, L), lambda b, c: (b, 0, c))
    return pl.pallas_call(
        _mlstm_kernel,
        grid=(batch, nc),
        in_specs=[row(ML_QK_W), tr(ML_QK_W), row(ML_V_W), row(ML_V_W), row(LANES), tr(GATE_ROWS),
                  pl.BlockSpec((1, ML_V_W), lambda b, c: (0, 0))],
        out_specs=row(ML_V_W),
        out_shape=jax.ShapeDtypeStruct((n, ML_V_W), BF16),
        scratch_shapes=[
            pltpu.VMEM((ML_QK_W, 2 * MLSTM_V_DIM), F32),
            pltpu.VMEM((8, LANES), F32),
        ],
        compiler_params=pltpu.CompilerParams(dimension_semantics=("arbitrary", "arbitrary"),
                                             vmem_limit_bytes=VMEM_LIMIT),
        name="mlstm",
    )(qm, kmt, vm, mo, gc, gt, g_out)


def _out_mlp_kernel(x_ref, att_ref, hm_ref, wo_ref, gpost_ref, gpre_ref, wup_ref, wdn_ref, gpost2_ref,
                    o_ref):
    mix = (jnp.dot(att_ref[...], wo_ref[:ATT_W, :], preferred_element_type=F32)
           + jnp.dot(hm_ref[...], wo_ref[ATT_W:, :], preferred_element_type=F32))
    x1 = x_ref[...] + _rms(mix, gpost_ref[...])
    hn = _rms(x1, gpre_ref[...]).astype(BF16)
    acc = jnp.zeros(x1.shape, F32)
    for c in range(D_FF // FF_CHUNK):
        cols = slice(c * FF_CHUNK, (c + 1) * FF_CHUNK)
        u = jnp.dot(hn, wup_ref[:, cols], preferred_element_type=F32)
        u = jnp.square(jnp.maximum(u, 0.0)).astype(BF16)
        acc = acc + jnp.dot(u, wdn_ref[cols, :], preferred_element_type=F32)
    o_ref[...] = x1 + _rms(acc, gpost2_ref[...])


def _out_mlp(x2d, att, hm, w_out, g_post, g_pre2, w_up, w_down, g_post2):
    n = x2d.shape[0]
    tm = MLP_ROW_TILE
    row = lambda w: pl.BlockSpec((tm, w), lambda i: (i, 0))
    full = lambda a: pl.BlockSpec(a.shape, lambda i: (0,) * a.ndim)
    return pl.pallas_call(
        _out_mlp_kernel,
        grid=(n // tm,),
        in_specs=[row(D_MODEL), row(ATT_W), row(ML_V_W), full(w_out), full(g_post), full(g_pre2),
                  full(w_up), full(w_down), full(g_post2)],
        out_specs=row(D_MODEL),
        out_shape=jax.ShapeDtypeStruct((n, D_MODEL), F32),
        compiler_params=pltpu.CompilerParams(dimension_semantics=("arbitrary",),
                                             vmem_limit_bytes=VMEM_LIMIT),
        name="out_mlp",
    )(x2d, att, hm, w_out, g_post, g_pre2, w_up, w_down, g_post2)


def kernel(x, positions, g_mix_pre, w_in, b_igate, b_fgate, g_mlstm_out, w_out, g_mix_post, g_mlp_pre,
           w_up, w_down, g_mlp_post):
    batch, seq, d = x.shape
    assert d == D_MODEL and w_in.shape[-1] == MAIN_W + 2 * MLSTM_HEADS
    assert seq % max(IN_ROW_TILE, MLP_ROW_TILE, TILES_PER_STEP * MOBA_BLOCK,
                     MLSTM_CHUNKS_PER_STEP * ROW_TILE) == 0
    depth = w_in.shape[0]
    cos, sa, sb = _rope_tables(positions)
    xs = x.reshape(batch * seq, d)
    pad = LANES - 2 * MLSTM_HEADS
    for layer in range(depth):
        w_gate = jnp.pad(w_in[layer, :, MAIN_W:], ((0, 0), (0, pad))).astype(BF16)
        b_gate = jnp.pad(jnp.concatenate([b_igate[layer], b_fgate[layer]]), (0, pad))[None, :].astype(F32)
        qat, ka, vat, qm, kmt, vm, mo, gc, gt = _in_proj(
            xs, g_mix_pre[layer][None, :], w_in, layer, w_gate, b_gate, cos, sa, sb, batch, seq)
        att = _moba(qat, ka, vat, batch, seq)
        hm = _mlstm(qm, kmt, vm, mo, gc, gt, g_mlstm_out[layer][None, :], batch, seq)
        xs = _out_mlp(xs, att, hm, w_out[layer].astype(BF16), g_mix_post[layer][None, :],
                      g_mlp_pre[layer][None, :], w_up[layer].astype(BF16), w_down[layer].astype(BF16),
                      g_mlp_post[layer][None, :])
    return xs.reshape(batch, seq, d)
```
